```python
import jax, jax.numpy as jnp
from jax import lax
import numpy as np

D_MODEL = 2048
BATCH = 2
SEQ = 16384
DEPTH = 1

GRID_W = 64
CTX_LEN = 256
LRU_WIDTH = D_MODEL // 2
LRU_HEADS = 8
LRU_HEAD_DIM = LRU_WIDTH // LRU_HEADS
LRU_C = 8.0
CONV_W = 4
CONV_LEFT = 2
SGU_WIDTH = D_MODEL // 2
SGU_GROUPS = 8
SGU_GROUP_DIM = SGU_WIDTH // SGU_GROUPS
CHUNK = 128
ROWS_PER_CHUNK = CHUNK // GRID_W
MIX_WIDTH = LRU_WIDTH + SGU_WIDTH
IN_WIDTH = 2 * LRU_WIDTH + 2 * SGU_WIDTH
PEER_HEADS = 8
PEER_NKEYS = 128
PEER_EXPERTS = PEER_NKEYS * PEER_NKEYS
PEER_QDIM = 256
PEER_HALF = PEER_QDIM // 2
PEER_TOPK = 16
PEER_BLOCK = 128
N_MOD = 6
EPS = 1e-6

kernel_name = "hybrid_rglru_sgu_peer_prefix_dit"


def rms_norm(x, g):
    xf = x.astype(jnp.float32)
    y = xf * lax.rsqrt(jnp.mean(xf * xf, axis=-1, keepdims=True) + EPS)
    return (y * g.astype(jnp.float32)).astype(x.dtype)


def layer_norm_noaffine(x):
    xf = x.astype(jnp.float32)
    mu = jnp.mean(xf, axis=-1, keepdims=True)
    var = jnp.mean(jnp.square(xf - mu), axis=-1, keepdims=True)
    return ((xf - mu) * lax.rsqrt(var + EPS)).astype(x.dtype)


def adaln(s, w_mod, b_mod, k):
    lo, hi = k * D_MODEL, (k + 1) * D_MODEL
    return (s @ w_mod[:, lo:hi] + b_mod[lo:hi])[..., None, :]


def modulate(h, shift, scale):
    return h * (1.0 + scale) + shift


def centred_conv(x, w, b):
    L = x.shape[1]
    xp = jnp.pad(x, ((0, 0), (CONV_LEFT, CONV_W - 1 - CONV_LEFT), (0, 0)))
    y = b
    for k in range(CONV_W):
        y = y + xp[:, k:k + L, :] * w[k]
    return y


def block_diag(x, w, b):
    B, L, _ = x.shape
    xh = x.reshape(B, L, LRU_HEADS, LRU_HEAD_DIM)
    return jnp.einsum('blhi,hij->blhj', xh, w).reshape(B, L, LRU_WIDTH) + b


def lru_coeffs(xc, wa, ba, wx, bx, lam):
    xf = xc.astype(jnp.float32)
    r = jax.nn.sigmoid(block_diag(xc, wa, ba).astype(jnp.float32))
    i = jax.nn.sigmoid(block_diag(xc, wx, bx).astype(jnp.float32))
    log_a = -LRU_C * r * jax.nn.softplus(-lam.astype(jnp.float32))
    a = jnp.exp(log_a)
    b = jnp.sqrt(-jnp.expm1(2.0 * log_a)) * (i * xf)
    return a, b


def linear_scan(a, b, h0, reverse):
    def step(h, ab):
        h = ab[0] * h + ab[1]
        return h, h
    h_fin, hs = lax.scan(step, h0, (jnp.swapaxes(a, 0, 1), jnp.swapaxes(b, 0, 1)), reverse=reverse)
    return h_fin, jnp.swapaxes(hs, 0, 1)


def scan_final_state(a, b, h0, reverse):
    def step(h, ab):
        return ab[0] * h + ab[1], None
    h_fin, _ = lax.scan(step, h0, (jnp.swapaxes(a, 0, 1), jnp.swapaxes(b, 0, 1)), reverse=reverse)
    return h_fin


def spatial_gating(u, v, w_s, b_s, n_chunks):
    B, L, _ = u.shape
    shp = (B, n_chunks, CHUNK, SGU_GROUPS, SGU_GROUP_DIM)
    ug = jax.nn.gelu(u).reshape(shp)
    vg = layer_norm_noaffine(jax.nn.gelu(v).reshape(shp))
    mixed = jnp.einsum('gpq,bnqgc->bnpgc', w_s, vg) + b_s.T[:, :, None]
    return (ug * mixed).reshape(B, L, SGU_WIDTH)


def mixer_output(proj, y_scan, sgu_w, sgu_b, w_out, n_chunks):
    gate = proj[..., LRU_WIDTH:2 * LRU_WIDTH]
    u = proj[..., 2 * LRU_WIDTH:2 * LRU_WIDTH + SGU_WIDTH]
    v = proj[..., 2 * LRU_WIDTH + SGU_WIDTH:]
    y_lru = jax.nn.gelu(gate) * y_scan.astype(proj.dtype)
    y_sgu = spatial_gating(u, v, sgu_w, sgu_b, n_chunks)
    return jnp.concatenate([y_lru, y_sgu], axis=-1) @ w_out


def peer_ffn(h, w_q, k1, k2, u_tab, v_tab):
    B, L, D = h.shape
    blocks = h.reshape((B * L) // PEER_BLOCK, PEER_BLOCK, D)

    def block_fn(hb):
        q = (hb @ w_q).reshape(PEER_BLOCK, PEER_HEADS, 2, PEER_HALF)
        s1 = jnp.einsum('thd,hkd->thk', q[:, :, 0], k1).astype(jnp.float32)
        s2 = jnp.einsum('thd,hkd->thk', q[:, :, 1], k2).astype(jnp.float32)
        v1, i1 = lax.top_k(s1, PEER_TOPK)
        v2, i2 = lax.top_k(s2, PEER_TOPK)
        cand = (v1[..., :, None] + v2[..., None, :]).reshape(PEER_BLOCK, PEER_HEADS, PEER_TOPK * PEER_TOPK)
        sc, flat = lax.top_k(cand, PEER_TOPK)
        e1 = jnp.take_along_axis(i1, flat // PEER_TOPK, axis=-1)
        e2 = jnp.take_along_axis(i2, flat % PEER_TOPK, axis=-1)
        idx = e1 * PEER_NKEYS + e2
        g = jax.nn.softmax(sc, axis=-1)
        u_sel = jnp.take(u_tab, idx, axis=0)
        act = jax.nn.gelu(jnp.einsum('td,thkd->thk', hb, u_sel))
        v_sel = jnp.take(v_tab, idx, axis=0)
        return jnp.einsum('thk,thkd->td', (g * act.astype(jnp.float32)).astype(hb.dtype), v_sel)

    return lax.map(block_fn, blocks).reshape(B, L, D)


def setup_inputs(seed: int = 0) -> dict:
    key = jax.random.key(seed)
    ks = jax.random.split(key, 32)
    f32 = jnp.float32
    D = D_MODEL

    def nrm(k, shape, scale):
        return jax.random.normal(k, shape, f32) * scale

    u_a = jax.random.uniform(ks[14], (DEPTH, 2, LRU_WIDTH), f32, 0.9, 0.999)
    s_a = u_a ** (1.0 / LRU_C)
    lru_lambda = jnp.log(s_a) - jnp.log1p(-s_a)
    return {
        "x": nrm(ks[0], (BATCH, SEQ, D), 1.0),
        "c": nrm(ks[1], (BATCH, D), 1.0),
        "ctx": nrm(ks[2], (BATCH, CTX_LEN, D), 1.0),
        "c_ctx": nrm(ks[3], (D,), 1.0),
        "w_mod": nrm(ks[4], (DEPTH, D, N_MOD * D), D ** -0.5),
        "b_mod": nrm(ks[5], (DEPTH, N_MOD * D), 0.02),
        "norm1_g": 1.0 + nrm(ks[6], (DEPTH, D), 0.02),
        "norm2_g": 1.0 + nrm(ks[7], (DEPTH, D), 0.02),
        "w_in": nrm(ks[8], (DEPTH, D, IN_WIDTH), D ** -0.5),
        "conv_w": nrm(ks[9], (DEPTH, CONV_W, LRU_WIDTH), CONV_W ** -0.5),
        "conv_b": nrm(ks[10], (DEPTH, LRU_WIDTH), 0.02),
        "lru_wa": nrm(ks[11], (DEPTH, 2, LRU_HEADS, LRU_HEAD_DIM, LRU_HEAD_DIM), LRU_HEAD_DIM ** -0.5),
        "lru_ba": nrm(ks[12], (DEPTH, 2, LRU_WIDTH), 0.02),
        "lru_wx": nrm(ks[13], (DEPTH, 2, LRU_HEADS, LRU_HEAD_DIM, LRU_HEAD_DIM), LRU_HEAD_DIM ** -0.5),
        "lru_bx": nrm(ks[15], (DEPTH, 2, LRU_WIDTH), 0.02),
        "lru_lambda": lru_lambda,
        "sgu_w": nrm(ks[16], (DEPTH, SGU_GROUPS, CHUNK, CHUNK), CHUNK ** -0.5),
        "sgu_b": nrm(ks[17], (DEPTH, SGU_GROUPS, CHUNK), 0.02),
        "w_out": nrm(ks[18], (DEPTH, MIX_WIDTH, D), MIX_WIDTH ** -0.5),
        "peer_wq": nrm(ks[19], (DEPTH, D, PEER_HEADS * PEER_QDIM), D ** -0.5),
        "peer_k1": nrm(ks[20], (DEPTH, PEER_HEADS, PEER_NKEYS, PEER_HALF), PEER_HALF ** -0.5),
        "peer_k2": nrm(ks[21], (DEPTH, PEER_HEADS, PEER_NKEYS, PEER_HALF), PEER_HALF ** -0.5),
        "peer_u": nrm(ks[22], (DEPTH, PEER_EXPERTS, D), D ** -0.5),
        "peer_v": nrm(ks[23], (DEPTH, PEER_EXPERTS, D), PEER_HEADS ** -0.5),
        "final_g": 1.0 + nrm(ks[24], (D,), 0.02),
    }


def reference(x, c, ctx, c_ctx, w_mod, b_mod, norm1_g, norm2_g, w_in, conv_w, conv_b,
              lru_wa, lru_ba, lru_wx, lru_bx, lru_lambda, sgu_w, sgu_b, w_out,
              peer_wq, peer_k1, peer_k2, peer_u, peer_v, final_g):
    B, L, _ = x.shape
    rows = L // GRID_W
    lat_chunks = rows // ROWS_PER_CHUNK
    ctx_chunks = ctx.shape[1] // CHUNK
    s_lat = jax.nn.silu(c)
    s_ctx = jax.nn.silu(c_ctx)
    x_lat, x_ctx = x, ctx
    for l in range(DEPTH):
        last = l == DEPTH - 1
        wm, bm = w_mod[l], b_mod[l]
        hn_lat = modulate(rms_norm(x_lat, norm1_g[l]), adaln(s_lat, wm, bm, 0), adaln(s_lat, wm, bm, 1))
        hn_ctx = modulate(rms_norm(x_ctx, norm1_g[l]), adaln(s_ctx, wm, bm, 0), adaln(s_ctx, wm, bm, 1))
        proj_lat = hn_lat @ w_in[l]
        if last:
            proj_ctx_x = hn_ctx @ w_in[l][:, :LRU_WIDTH]
        else:
            proj_ctx = hn_ctx @ w_in[l]
            proj_ctx_x = proj_ctx[..., :LRU_WIDTH]
        xc_lat = centred_conv(proj_lat[..., :LRU_WIDTH], conv_w[l], conv_b[l])
        xc_ctx = centred_conv(proj_ctx_x, conv_w[l], conv_b[l])
        y_scan_lat = jnp.zeros((B, L, LRU_WIDTH), jnp.float32)
        y_scan_ctx = jnp.zeros((B, x_ctx.shape[1], LRU_WIDTH), jnp.float32)
        for d, rev in enumerate((False, True)):
            prm = (lru_wa[l, d], lru_ba[l, d], lru_wx[l, d], lru_bx[l, d], lru_lambda[l, d])
            a_c, b_c = lru_coeffs(xc_ctx, *prm)
            h0 = jnp.zeros((B, LRU_WIDTH), jnp.float32)
            if last:
                h_ctx = scan_final_state(a_c, b_c, h0, rev)
            else:
                h_ctx, ys_c = linear_scan(a_c, b_c, h0, rev)
                y_scan_ctx = y_scan_ctx + ys_c
            a_l, b_l = lru_coeffs(xc_lat, *prm)
            _, ys_l = linear_scan(a_l, b_l, h_ctx, rev)
            y_scan_lat = y_scan_lat + ys_l
        y_lat = mixer_output(proj_lat, y_scan_lat, sgu_w[l], sgu_b[l], w_out[l], lat_chunks)
        x_lat = x_lat + adaln(s_lat, wm, bm, 2) * y_lat
        hn2_lat = modulate(rms_norm(x_lat, norm2_g[l]), adaln(s_lat, wm, bm, 3), adaln(s_lat, wm, bm, 4))
        x_lat = x_lat + adaln(s_lat, wm, bm, 5) * peer_ffn(hn2_lat, peer_wq[l], peer_k1[l], peer_k2[l], peer_u[l], peer_v[l])
        if not last:
            y_ctx = mixer_output(proj_ctx, y_scan_ctx, sgu_w[l], sgu_b[l], w_out[l], ctx_chunks)
            x_ctx = x_ctx + adaln(s_ctx, wm, bm, 2) * y_ctx
            hn2_ctx = modulate(rms_norm(x_ctx, norm2_g[l]), adaln(s_ctx, wm, bm, 3), adaln(s_ctx, wm, bm, 4))
            x_ctx = x_ctx + adaln(s_ctx, wm, bm, 5) * peer_ffn(hn2_ctx, peer_wq[l], peer_k1[l], peer_k2[l], peer_u[l], peer_v[l])
    return rms_norm(x_lat, final_g)
```

```python
import functools

import jax
import jax.numpy as jnp
from jax import lax
from jax.experimental import pallas as pl
from jax.experimental.pallas import tpu as pltpu

F32 = jnp.float32
BF16 = jnp.bfloat16

EPS = 1e-6
LRU_C = 8.0
CONV_LEFT = 2
CHUNK = 128
PEER_TOPK = 16
N_MOD = 6

V7X_LANES = 128
V7X_SUBLANES = 8
V7X_VMEM_LIMIT_BYTES = 56 * 1024 * 1024


def _params(*semantics):
    return pltpu.CompilerParams(dimension_semantics=semantics, vmem_limit_bytes=V7X_VMEM_LIMIT_BYTES)


def _gelu(x):
    return 0.5 * x * (1.0 + jnp.tanh(0.7978845608028654 * (x + 0.044715 * (x * x * x))))


def _norm_modulate(x, g, shift, scale):
    ms = jnp.mean(x * x, axis=-1, keepdims=True)
    return (x * lax.rsqrt(ms + EPS) * g) * (1.0 + scale) + shift


def _modulation_kernel(cs_ref, w_ref, b_ref, o_ref):
    n_rows = cs_ref.shape[0]
    nb = w_ref.shape[1]
    rows = []
    for r in range(n_rows):
        c = cs_ref[r]
        s = c * jax.nn.sigmoid(c)
        parts = [jnp.sum(w_ref[:, n * V7X_LANES:(n + 1) * V7X_LANES] * s, axis=0, keepdims=True)
                 for n in range(nb // V7X_LANES)]
        rows.append(jnp.concatenate(parts, axis=1) + b_ref[...])
    rows.append(jnp.zeros((V7X_SUBLANES - n_rows, nb), F32))
    o_ref[...] = jnp.concatenate(rows, axis=0)


def _modulation(cs, w_mod, b_mod):
    n_rows, d = cs.shape
    n = w_mod.shape[1]
    nb = 1024
    csb = jnp.broadcast_to(cs[:, :, None], (n_rows, d, V7X_LANES))
    out = pl.pallas_call(
        _modulation_kernel,
        grid=(n // nb,),
        in_specs=[
            pl.BlockSpec((n_rows, d, V7X_LANES), lambda j: (0, 0, 0)),
            pl.BlockSpec((d, nb), lambda j: (0, j)),
            pl.BlockSpec((1, nb), lambda j: (0, j)),
        ],
        out_specs=pl.BlockSpec((V7X_SUBLANES, nb), lambda j: (0, j)),
        out_shape=jax.ShapeDtypeStruct((V7X_SUBLANES, n), F32),
        compiler_params=_params("arbitrary"),
        name="modulation",
    )(csb, w_mod, b_mod.reshape(1, n))
    return out[:n_rows]


def _norm_proj_kernel(x_ref, g_ref, shift_ref, scale_ref, w_ref, o_ref):
    hn = _norm_modulate(x_ref[...], g_ref[...], shift_ref[0], scale_ref[0])
    o_ref[...] = jnp.dot(hn.astype(BF16), w_ref[...], preferred_element_type=F32)


def _norm_proj(x2, g, shift, scale, w, rows_per_vec, tm):
    r, d = x2.shape
    n = w.shape[1]
    tn = 1024
    tiles_per_vec = rows_per_vec // tm
    return pl.pallas_call(
        _norm_proj_kernel,
        grid=(r // tm, n // tn),
        in_specs=[
            pl.BlockSpec((tm, d), lambda i, j: (i, 0)),
            pl.BlockSpec((1, d), lambda i, j: (0, 0)),
            pl.BlockSpec((1, 1, d), lambda i, j: (i // tiles_per_vec, 0, 0)),
            pl.BlockSpec((1, 1, d), lambda i, j: (i // tiles_per_vec, 0, 0)),
            pl.BlockSpec((d, tn), lambda i, j: (0, j)),
        ],
        out_specs=pl.BlockSpec((tm, tn), lambda i, j: (i, j)),
        out_shape=jax.ShapeDtypeStruct((r, n), F32),
        compiler_params=_params("arbitrary", "arbitrary"),
        name="norm_proj",
    )(x2, g, shift, scale, w)


def _lru_scan_kernel(cur_ref, prev_ref, next_ref, cw_ref, cb_ref, wa_ref, ba_ref, wx_ref, bx_ref,
                     lam_ref, h0_ref, y_ref, xe_ref, a_ref, b_ref, h_ref, *, reverse, n_tiles):
    i = pl.program_id(1)
    ti = (n_tiles - 1 - i) if reverse else i
    t_rows, width = a_ref.shape
    heads, head_dim = wa_ref.shape[0], wa_ref.shape[1]
    halo = V7X_SUBLANES

    @pl.when(i == 0)
    def _():
        h_ref[...] = h0_ref[0]

    zero = jnp.zeros((halo, width), F32)
    xe_ref[0:halo, :] = jnp.where(ti == 0, zero, prev_ref[0])
    xe_ref[halo:halo + t_rows, :] = cur_ref[0]
    xe_ref[halo + t_rows:, :] = jnp.where(ti == n_tiles - 1, zero, next_ref[0])

    xc = jnp.broadcast_to(cb_ref[...], (t_rows, width))
    for k in range(cw_ref.shape[0]):
        lo = halo - CONV_LEFT + k
        xc = xc + xe_ref[lo:lo + t_rows, :] * cw_ref[k:k + 1, :]

    xcb = xc.astype(BF16)
    r_parts, i_parts = [], []
    for h in range(heads):
        xh = xcb[:, h * head_dim:(h + 1) * head_dim]
        r_parts.append(jnp.dot(xh, wa_ref[h], preferred_element_type=F32))
        i_parts.append(jnp.dot(xh, wx_ref[h], preferred_element_type=F32))
    r = jax.nn.sigmoid(jnp.concatenate(r_parts, axis=1) + ba_ref[...])
    ig = jax.nn.sigmoid(jnp.concatenate(i_parts, axis=1) + bx_ref[...])
    lam = lam_ref[...]
    softplus_neg_lam = jnp.maximum(-lam, 0.0) + jnp.log1p(jnp.exp(-jnp.abs(lam)))
    log_a = -LRU_C * r * softplus_neg_lam
    a_ref[...] = jnp.exp(log_a)
    th = jnp.tanh(log_a)
    b_ref[...] = jnp.sqrt(-2.0 * th / (1.0 - th)) * (ig * xc)

    def step(s, h):
        t = (t_rows - 1 - s) if reverse else s
        h = a_ref[pl.ds(t, 1), :] * h + b_ref[pl.ds(t, 1), :]
        y_ref[0, pl.ds(t, 1), :] = h
        return h

    h_ref[...] = lax.fori_loop(0, t_rows, step, h_ref[...], unroll=8)


def _lru_scan(px, conv_w, conv_b, wa, ba, wx, bx, lam, h0, reverse, t_rows):
    bsz, seq, _ = px.shape
    width = conv_w.shape[1]
    n_tiles = seq // t_rows
    halo = V7X_SUBLANES
    tb = t_rows // halo
    last_halo_block = seq // halo - 1

    def tile(i):
        return (n_tiles - 1 - i) if reverse else i

    vec = lambda b, i: (0, 0)
    kern = functools.partial(_lru_scan_kernel, reverse=reverse, n_tiles=n_tiles)
    return pl.pallas_call(
        kern,
        grid=(bsz, n_tiles),
        in_specs=[
            pl.BlockSpec((1, t_rows, width), lambda b, i: (b, tile(i), 0)),
            pl.BlockSpec((1, halo, width), lambda b, i: (b, jnp.maximum(tile(i) * tb - 1, 0), 0)),
            pl.BlockSpec((1, halo, width), lambda b, i: (b, jnp.minimum((tile(i) + 1) * tb, last_halo_block), 0)),
            pl.BlockSpec(conv_w.shape, vec),
            pl.BlockSpec((1, width), vec),
            pl.BlockSpec(wa.shape, lambda b, i: (0, 0, 0)),
            pl.BlockSpec((1, width), vec),
            pl.BlockSpec(wx.shape, lambda b, i: (0, 0, 0)),
            pl.BlockSpec((1, width), vec),
            pl.BlockSpec((1, width), vec),
            pl.BlockSpec((1, 1, width), lambda b, i: (b, 0, 0)),
        ],
        out_specs=pl.BlockSpec((1, t_rows, width), lambda b, i: (b, tile(i), 0)),
        out_shape=jax.ShapeDtypeStruct((bsz, seq, width), F32),
        scratch_shapes=[
            pltpu.VMEM((t_rows + 2 * halo, width), F32),
            pltpu.VMEM((t_rows, width), F32),
            pltpu.VMEM((t_rows, width), F32),
            pltpu.VMEM((1, width), F32),
        ],
        compiler_params=_params("arbitrary", "arbitrary"),
        name="lru_scan_bwd" if reverse else "lru_scan_fwd",
    )(px, px, px, conv_w, conv_b.reshape(1, width), wa, ba.reshape(1, width), wx, bx.reshape(1, width),
      lam.reshape(1, width), h0)


def _mixer_kernel(x_ref, gate_ref, u_ref, v_ref, yf_ref, yb_ref, sw_ref, sb_ref, wout_ref, g1_ref, o_ref,
                  ycat_ref):
    t_rows = x_ref.shape[1]
    lru_width = gate_ref.shape[2]
    groups, group_dim = sw_ref.shape[0], u_ref.shape[2] // sw_ref.shape[0]

    ycat_ref[:, 0:lru_width] = (_gelu(gate_ref[0]) * (yf_ref[0] + yb_ref[0])).astype(BF16)
    for n in range(t_rows // CHUNK):
        rows = slice(n * CHUNK, (n + 1) * CHUNK)
        for g in range(groups):
            cols = slice(g * group_dim, (g + 1) * group_dim)
            vg = _gelu(v_ref[0, rows, cols])
            mu = jnp.mean(vg, axis=-1, keepdims=True)
            dv = vg - mu
            var = jnp.mean(dv * dv, axis=-1, keepdims=True)
            vn = dv * lax.rsqrt(var + EPS)
            mixed = jnp.dot(sw_ref[g], vn.astype(BF16), preferred_element_type=F32) + sb_ref[g]
            ycat_ref[rows, lru_width + g * group_dim:lru_width + (g + 1) * group_dim] = (
                _gelu(u_ref[0, rows, cols]) * mixed).astype(BF16)
    y = jnp.dot(ycat_ref[...], wout_ref[...], preferred_element_type=F32)
    o_ref[0] = x_ref[0] + g1_ref[0] * y


def _mixer(x, proj, y_fwd, y_bwd, sgu_w, sgu_b_bcast, w_out, gate1, t_rows):
    bsz, seq, d = x.shape
    width = y_fwd.shape[2]
    tile3 = lambda c: pl.BlockSpec((1, t_rows, width), lambda b, i: (b, i, c))
    return pl.pallas_call(
        _mixer_kernel,
        grid=(bsz, seq // t_rows),
        in_specs=[
            pl.BlockSpec((1, t_rows, d), lambda b, i: (b, i, 0)),
            tile3(1), tile3(2), tile3(3),
            tile3(0), tile3(0),
            pl.BlockSpec(sgu_w.shape, lambda b, i: (0, 0, 0)),
            pl.BlockSpec(sgu_b_bcast.shape, lambda b, i: (0, 0, 0)),
            pl.BlockSpec(w_out.shape, lambda b, i: (0, 0)),
            pl.BlockSpec((1, 1, d), lambda b, i: (b, 0, 0)),
        ],
        out_specs=pl.BlockSpec((1, t_rows, d), lambda b, i: (b, i, 0)),
        out_shape=jax.ShapeDtypeStruct((bsz, seq, d), F32),
        scratch_shapes=[pltpu.VMEM((t_rows, w_out.shape[0]), BF16)],
        compiler_params=_params("arbitrary", "arbitrary"),
        name="mixer",
    )(x, proj, proj, proj, y_fwd, y_bwd, sgu_w, sgu_b_bcast, w_out, gate1)


def _top16_sorted(s):
    n_tok = s.shape[1]
    neg_inf = jnp.float32(-jnp.inf)
    slot = lax.broadcasted_iota(jnp.int32, (PEER_TOPK, n_tok), 0).astype(F32)
    top = jnp.full((PEER_TOPK, n_tok), neg_inf, F32)
    taken = jnp.zeros((1, n_tok), F32)
    x = s
    for _ in range(PEER_TOPK):
        m = jnp.max(x, axis=0, keepdims=True)
        eq = x == m
        c = jnp.sum(jnp.where(eq, 1.0, 0.0), axis=0, keepdims=True)
        top = jnp.where((slot >= taken) & (slot < taken + c), m, top)
        taken = taken + c
        x = jnp.where(eq, neg_inf, x)
    return top


def _kth_largest(x, k):
    n_tok = x.shape[1]
    neg_inf = jnp.float32(-jnp.inf)
    tau = jnp.full((1, n_tok), neg_inf, F32)
    taken = jnp.zeros((1, n_tok), F32)
    for _ in range(k):
        m = jnp.max(x, axis=0, keepdims=True)
        eq = x == m
        tau = jnp.where(taken < k, m, tau)
        taken = taken + jnp.sum(jnp.where(eq, 1.0, 0.0), axis=0, keepdims=True)
        x = jnp.where(eq, neg_inf, x)
    return tau


def _pair_candidates(v1, v2):
    n_tok = v1.shape[1]
    sub = V7X_SUBLANES
    neg_inf = jnp.float32(-jnp.inf)
    j_idx = lax.broadcasted_iota(jnp.int32, (sub, n_tok), 0)
    groups = [v1[0:1] + v2[0:sub], v1[0:1] + v2[sub:2 * sub], v1[1:2] + v2[0:sub]]
    for i in range(2, sub):
        groups.append(jnp.where(j_idx < PEER_TOPK // (i + 1), v1[i:i + 1] + v2[0:sub], neg_inf))
    groups.append(v1[sub:2 * sub] + v2[0:1])
    return jnp.concatenate(groups, axis=0)


def _peer_route_kernel(x_ref, g_ref, shift_ref, scale_ref, wqt_ref, k1_ref, k2_ref,
                       ht_ref, s1_ref, s2_ref, st_ref):
    heads, n_keys, half = k1_ref.shape
    n_tok = x_ref.shape[0]
    hn = _norm_modulate(x_ref[...], g_ref[...], shift_ref[0], scale_ref[0])
    ht = hn.T.astype(BF16)
    ht_ref[...] = ht
    qt = jnp.dot(wqt_ref[...], ht, preferred_element_type=F32)
    for h in range(heads):
        q1 = qt[(2 * h) * half:(2 * h + 1) * half].astype(BF16)
        q2 = qt[(2 * h + 1) * half:(2 * h + 2) * half].astype(BF16)
        s1 = jnp.dot(k1_ref[h], q1, preferred_element_type=F32)
        s2 = jnp.dot(k2_ref[h], q2, preferred_element_type=F32)
        s1_ref[h] = s1
        s2_ref[h] = s2
        v1 = _top16_sorted(s1)
        v2 = _top16_sorted(s2)
        cand = _pair_candidates(v1, v2)
        tau = _kth_largest(cand, PEER_TOPK)
        m1, m2 = v1[0:1], v2[0:1]
        z = jnp.sum(jnp.where(cand >= tau, jnp.exp(cand - (m1 + m2)), 0.0), axis=0, keepdims=True)
        st_ref[h] = jnp.concatenate(
            [tau, m1, m2, 1.0 / z, jnp.zeros((V7X_SUBLANES - 4, n_tok), F32)], axis=0)


def _peer_route(x2, g, shift, scale, wq_t, k1, k2, rows_per_vec, tm):
    n, d = x2.shape
    heads, n_keys, _ = k1.shape
    tiles_per_vec = rows_per_vec // tm
    return pl.pallas_call(
        _peer_route_kernel,
        grid=(n // tm,),
        in_specs=[
            pl.BlockSpec((tm, d), lambda i: (i, 0)),
            pl.BlockSpec((1, d), lambda i: (0, 0)),
            pl.BlockSpec((1, 1, d), lambda i: (i // tiles_per_vec, 0, 0)),
            pl.BlockSpec((1, 1, d), lambda i: (i // tiles_per_vec, 0, 0)),
            pl.BlockSpec(wq_t.shape, lambda i: (0, 0)),
            pl.BlockSpec(k1.shape, lambda i: (0, 0, 0)),
            pl.BlockSpec(k2.shape, lambda i: (0, 0, 0)),
        ],
        out_specs=[
            pl.BlockSpec((d, tm), lambda i: (0, i)),
            pl.BlockSpec((heads, n_keys, tm), lambda i: (0, 0, i)),
            pl.BlockSpec((heads, n_keys, tm), lambda i: (0, 0, i)),
            pl.BlockSpec((heads, V7X_SUBLANES, tm), lambda i: (0, 0, i)),
        ],
        out_shape=[
            jax.ShapeDtypeStruct((d, n), BF16),
            jax.ShapeDtypeStruct((heads, n_keys, n), F32),
            jax.ShapeDtypeStruct((heads, n_keys, n), F32),
            jax.ShapeDtypeStruct((heads, V7X_SUBLANES, n), F32),
        ],
        compiler_params=_params("arbitrary"),
        name="peer_route",
    )(x2, g, shift, scale, wq_t, k1, k2)


def _peer_experts_kernel(ht_ref, s1_ref, s2_ref, st_ref, u_ref, vt_ref, o_ref, p1_ref, p2_ref, wa_ref, acc_ref):
    e = pl.program_id(1)
    heads, n_keys, _ = s1_ref.shape
    e1_per_block = u_ref.shape[0] // n_keys

    @pl.when(e == 0)
    def _():
        acc_ref[...] = jnp.zeros_like(acc_ref)
        for h in range(heads):
            p1_ref[h] = jnp.exp(s1_ref[h] - st_ref[h, 1:2, :]) * st_ref[h, 3:4, :]
            p2_ref[h] = jnp.exp(s2_ref[h] - st_ref[h, 2:3, :])

    a_t = jnp.dot(u_ref[...], ht_ref[...], preferred_element_type=F32)
    for j in range(e1_per_block):
        e1 = e * e1_per_block + j
        rows = slice(j * n_keys, (j + 1) * n_keys)
        w = None
        for h in range(heads):
            sc = s2_ref[h] + s1_ref[h, pl.ds(e1, 1), :]
            wh = jnp.where(sc >= st_ref[h, 0:1, :], p2_ref[h] * p1_ref[h, pl.ds(e1, 1), :], 0.0)
            w = wh if w is None else w + wh
        wa_ref[rows, :] = (w * _gelu(a_t[rows])).astype(BF16)
    acc_ref[...] += jnp.dot(vt_ref[...], wa_ref[...], preferred_element_type=F32)

    @pl.when(e == pl.num_programs(1) - 1)
    def _():
        o_ref[...] = acc_ref[...].T


def _peer_experts(ht, s1, s2, st, u_bf, vt_bf, tm, eb):
    d, n = ht.shape
    heads, n_keys, _ = s1.shape
    n_exp = u_bf.shape[0]
    return pl.pallas_call(
        _peer_experts_kernel,
        grid=(n // tm, n_exp // eb),
        in_specs=[
            pl.BlockSpec((d, tm), lambda i, e: (0, i)),
            pl.BlockSpec((heads, n_keys, tm), lambda i, e: (0, 0, i)),
            pl.BlockSpec((heads, n_keys, tm), lambda i, e: (0, 0, i)),
            pl.BlockSpec((heads, V7X_SUBLANES, tm), lambda i, e: (0, 0, i)),
            pl.BlockSpec((eb, d), lambda i, e: (e, 0)),
            pl.BlockSpec((d, eb), lambda i, e: (0, e)),
        ],
        out_specs=pl.BlockSpec((tm, d), lambda i, e: (i, 0)),
        out_shape=jax.ShapeDtypeStruct((n, d), F32),
        scratch_shapes=[
            pltpu.VMEM((heads, n_keys, tm), F32),
            pltpu.VMEM((heads, n_keys, tm), F32),
            pltpu.VMEM((eb, tm), BF16),
            pltpu.VMEM((d, tm), F32),
        ],
        compiler_params=_params("arbitrary", "arbitrary"),
        name="peer_experts",
    )(ht, s1, s2, st, u_bf, vt_bf)


def _final_kernel(x_ref, p_ref, gate_ref, g_ref, o_ref):
    x = x_ref[...] + gate_ref[0] * p_ref[...]
    ms = jnp.mean(x * x, axis=-1, keepdims=True)
    o_ref[...] = x * lax.rsqrt(ms + EPS) * g_ref[...]


def _final(x2, peer, gate, g, rows_per_vec, tm):
    n, d = x2.shape
    tiles_per_vec = rows_per_vec // tm
    return pl.pallas_call(
        _final_kernel,
        grid=(n // tm,),
        in_specs=[
            pl.BlockSpec((tm, d), lambda i: (i, 0)),
            pl.BlockSpec((tm, d), lambda i: (i, 0)),
            pl.BlockSpec((1, 1, d), lambda i: (i // tiles_per_vec, 0, 0)),
            pl.BlockSpec((1, d), lambda i: (0, 0)),
        ],
        out_specs=pl.BlockSpec((tm, d), lambda i: (i, 0)),
        out_shape=jax.ShapeDtypeStruct((n, d), F32),
        compiler_params=_params("arbitrary"),
        name="final_norm",
    )(x2, peer, gate, g)


def kernel(x, c, ctx, c_ctx, w_mod, b_mod, norm1_g, norm2_g, w_in, conv_w, conv_b, lru_wa, lru_ba, lru_wx,
           lru_bx, lru_lambda, sgu_w, sgu_b, w_out, peer_wq, peer_k1, peer_k2, peer_u, peer_v, final_g):
    bsz, seq, d = x.shape
    ctx_len = ctx.shape[1]
    depth = w_mod.shape[0]
    assert depth == 1, "only the single-layer configuration of the reference is implemented"
    lru_width = conv_w.shape[2]
    assert seq % 512 == 0 and ctx_len % CHUNK == 0 and d % V7X_LANES == 0

    t_seq = 512
    t_tok = 512
    expert_block = 1024

    w_in_bf = w_in[0].astype(BF16)
    w_out_bf = w_out[0].astype(BF16)
    wa_bf, wx_bf = lru_wa[0].astype(BF16), lru_wx[0].astype(BF16)
    sgu_w_bf = sgu_w[0].astype(BF16)
    sgu_b_bcast = jnp.broadcast_to(sgu_b[0][:, :, None], sgu_b.shape[1:] + (sgu_w.shape[-1],))
    wq_t_bf = peer_wq[0].T.astype(BF16)
    k1_bf, k2_bf = peer_k1[0].astype(BF16), peer_k2[0].astype(BF16)
    u_bf = peer_u[0].astype(BF16)
    vt_bf = peer_v[0].T.astype(BF16)

    mod = _modulation(jnp.concatenate([c, c_ctx[None, :]], axis=0), w_mod[0], b_mod[0])
    mod = mod.reshape(bsz + 1, N_MOD, 1, d)
    lat = lambda k: mod[:bsz, k]
    ctx_vec = lambda k: mod[bsz:, k]
    g1 = norm1_g[0].reshape(1, d)
    g2 = norm2_g[0].reshape(1, d)

    px_ctx = _norm_proj(ctx.reshape(bsz * ctx_len, d), g1, ctx_vec(0), ctx_vec(1), w_in_bf[:, :lru_width],
                        rows_per_vec=bsz * ctx_len, tm=ctx_len).reshape(bsz, ctx_len, lru_width)
    proj = _norm_proj(x.reshape(bsz * seq, d), g1, lat(0), lat(1), w_in_bf,
                      rows_per_vec=seq, tm=1024).reshape(bsz, seq, w_in.shape[2])

    zeros_h = jnp.zeros((bsz, 1, lru_width), F32)
    ys = []
    for direction, reverse in enumerate((False, True)):
        prm = (conv_w[0], conv_b[0], wa_bf[direction], lru_ba[0, direction], wx_bf[direction],
               lru_bx[0, direction], lru_lambda[0, direction])
        y_ctx = _lru_scan(px_ctx, *prm, zeros_h, reverse, ctx_len)
        h_ctx = y_ctx[:, 0:1] if reverse else y_ctx[:, ctx_len - 1:ctx_len]
        ys.append(_lru_scan(proj, *prm, h_ctx, reverse, t_seq))

    x_mid = _mixer(x, proj, ys[0], ys[1], sgu_w_bf, sgu_b_bcast, w_out_bf, lat(2), t_seq)

    x2 = x_mid.reshape(bsz * seq, d)
    ht, s1, s2, st = _peer_route(x2, g2, lat(3), lat(4), wq_t_bf, k1_bf, k2_bf, rows_per_vec=seq, tm=t_tok)
    peer = _peer_experts(ht, s1, s2, st, u_bf, vt_bf, t_tok, expert_block)
    out = _final(x2, peer, lat(5), final_g.reshape(1, d), rows_per_vec=seq, tm=512)
    return out.reshape(bsz, seq, d)
```

```python
import functools

import jax
import jax.numpy as jnp
from jax import lax
from jax.experimental import pallas as pl
from jax.experimental.pallas import tpu as pltpu

F32 = jnp.float32
BF16 = jnp.bfloat16
U32 = jnp.uint32

EPS = 1e-6
LRU_C = 8.0
CONV_LEFT = 2
CHUNK = 128
PEER_TOPK = 16
N_MOD = 6

V7X_LANES = 128
V7X_SUBLANES = 8
V7X_BF16_ROWS = 16
V7X_MXU_COLS = 256
V7X_VMEM_LIMIT_BYTES = 56 * 1024 * 1024


def _params(*semantics):
    return pltpu.CompilerParams(dimension_semantics=semantics, vmem_limit_bytes=V7X_VMEM_LIMIT_BYTES)


def _gelu(x):
    return 0.5 * x * (1.0 + jnp.tanh(0.7978845608028654 * (x + 0.044715 * (x * x * x))))


def _norm_modulate(x, g, shift, scale):
    ms = jnp.mean(x * x, axis=-1, keepdims=True)
    return (x * lax.rsqrt(ms + EPS) * g) * (1.0 + scale) + shift


def _modulation_kernel(cs_ref, w_ref, b_ref, o_ref):
    n_rows = cs_ref.shape[0]
    nb = w_ref.shape[1]
    rows = []
    for r in range(n_rows):
        c = cs_ref[r]
        s = c * jax.nn.sigmoid(c)
        parts = [jnp.sum(w_ref[:, n * V7X_LANES:(n + 1) * V7X_LANES] * s, axis=0, keepdims=True)
                 for n in range(nb // V7X_LANES)]
        rows.append(jnp.concatenate(parts, axis=1) + b_ref[...])
    rows.append(jnp.zeros((V7X_SUBLANES - n_rows, nb), F32))
    o_ref[...] = jnp.concatenate(rows, axis=0)


def _modulation(cs, w_mod, b_mod):
    n_rows, d = cs.shape
    n = w_mod.shape[1]
    nb = 1024
    csb = jnp.broadcast_to(cs[:, :, None], (n_rows, d, V7X_LANES))
    out = pl.pallas_call(
        _modulation_kernel,
        grid=(n // nb,),
        in_specs=[
            pl.BlockSpec((n_rows, d, V7X_LANES), lambda j: (0, 0, 0)),
            pl.BlockSpec((d, nb), lambda j: (0, j)),
            pl.BlockSpec((1, nb), lambda j: (0, j)),
        ],
        out_specs=pl.BlockSpec((V7X_SUBLANES, nb), lambda j: (0, j)),
        out_shape=jax.ShapeDtypeStruct((V7X_SUBLANES, n), F32),
        compiler_params=_params("arbitrary"),
        name="modulation",
    )(csb, w_mod, b_mod.reshape(1, n))
    return out[:n_rows]


def _norm_proj_kernel(x_ref, g_ref, shift_ref, scale_ref, w_ref, o_ref):
    hn = _norm_modulate(x_ref[...], g_ref[...], shift_ref[0], scale_ref[0])
    o_ref[...] = jnp.dot(hn.astype(BF16), w_ref[...], preferred_element_type=F32)


def _norm_proj(x2, g, shift, scale, w, rows_per_vec, tm):
    r, d = x2.shape
    n = w.shape[1]
    tn = 1024
    tiles_per_vec = rows_per_vec // tm
    return pl.pallas_call(
        _norm_proj_kernel,
        grid=(r // tm, n // tn),
        in_specs=[
            pl.BlockSpec((tm, d), lambda i, j: (i, 0)),
            pl.BlockSpec((1, d), lambda i, j: (0, 0)),
            pl.BlockSpec((1, 1, d), lambda i, j: (i // tiles_per_vec, 0, 0)),
            pl.BlockSpec((1, 1, d), lambda i, j: (i // tiles_per_vec, 0, 0)),
            pl.BlockSpec((d, tn), lambda i, j: (0, j)),
        ],
        out_specs=pl.BlockSpec((tm, tn), lambda i, j: (i, j)),
        out_shape=jax.ShapeDtypeStruct((r, n), F32),
        compiler_params=_params("arbitrary", "arbitrary"),
        name="norm_proj",
    )(x2, g, shift, scale, w)


def _lru_scan_kernel(cur_ref, prev_ref, next_ref, cw_ref, cb_ref, wa_ref, ba_ref, wx_ref, bx_ref,
                     lam_ref, h0_ref, y_ref, xe_ref, a_ref, b_ref, h_ref, *, reverse, n_tiles):
    i = pl.program_id(1)
    ti = (n_tiles - 1 - i) if reverse else i
    t_rows, width = a_ref.shape
    heads, head_dim = wa_ref.shape[0], wa_ref.shape[1]
    halo = V7X_SUBLANES

    @pl.when(i == 0)
    def _():
        h_ref[...] = h0_ref[0]

    zero = jnp.zeros((halo, width), F32)
    xe_ref[0:halo, :] = jnp.where(ti == 0, zero, prev_ref[0])
    xe_ref[halo:halo + t_rows, :] = cur_ref[0]
    xe_ref[halo + t_rows:, :] = jnp.where(ti == n_tiles - 1, zero, next_ref[0])

    xc = jnp.broadcast_to(cb_ref[...], (t_rows, width))
    for k in range(cw_ref.shape[0]):
        lo = halo - CONV_LEFT + k
        xc = xc + xe_ref[lo:lo + t_rows, :] * cw_ref[k:k + 1, :]

    xcb = xc.astype(BF16)
    r_parts, i_parts = [], []
    for h in range(heads):
        xh = xcb[:, h * head_dim:(h + 1) * head_dim]
        r_parts.append(jnp.dot(xh, wa_ref[h], preferred_element_type=F32))
        i_parts.append(jnp.dot(xh, wx_ref[h], preferred_element_type=F32))
    r = jax.nn.sigmoid(jnp.concatenate(r_parts, axis=1) + ba_ref[...])
    ig = jax.nn.sigmoid(jnp.concatenate(i_parts, axis=1) + bx_ref[...])
    lam = lam_ref[...]
    softplus_neg_lam = jnp.maximum(-lam, 0.0) + jnp.log1p(jnp.exp(-jnp.abs(lam)))
    log_a = -LRU_C * r * softplus_neg_lam
    a_ref[...] = jnp.exp(log_a)
    th = jnp.tanh(log_a)
    b_ref[...] = jnp.sqrt(-2.0 * th / (1.0 - th)) * (ig * xc)

    def step(s, h):
        t = (t_rows - 1 - s) if reverse else s
        h = a_ref[pl.ds(t, 1), :] * h + b_ref[pl.ds(t, 1), :]
        y_ref[0, pl.ds(t, 1), :] = h
        return h

    h_ref[...] = lax.fori_loop(0, t_rows, step, h_ref[...], unroll=8)


def _lru_scan(px, conv_w, conv_b, wa, ba, wx, bx, lam, h0, reverse, t_rows):
    bsz, seq, _ = px.shape
    width = conv_w.shape[1]
    n_tiles = seq // t_rows
    halo = V7X_SUBLANES
    tb = t_rows // halo
    last_halo_block = seq // halo - 1

    def tile(i):
        return (n_tiles - 1 - i) if reverse else i

    vec = lambda b, i: (0, 0)
    kern = functools.partial(_lru_scan_kernel, reverse=reverse, n_tiles=n_tiles)
    return pl.pallas_call(
        kern,
        grid=(bsz, n_tiles),
        in_specs=[
            pl.BlockSpec((1, t_rows, width), lambda b, i: (b, tile(i), 0)),
            pl.BlockSpec((1, halo, width), lambda b, i: (b, jnp.maximum(tile(i) * tb - 1, 0), 0)),
            pl.BlockSpec((1, halo, width), lambda b, i: (b, jnp.minimum((tile(i) + 1) * tb, last_halo_block), 0)),
            pl.BlockSpec(conv_w.shape, vec),
            pl.BlockSpec((1, width), vec),
            pl.BlockSpec(wa.shape, lambda b, i: (0, 0, 0)),
            pl.BlockSpec((1, width), vec),
            pl.BlockSpec(wx.shape, lambda b, i: (0, 0, 0)),
            pl.BlockSpec((1, width), vec),
            pl.BlockSpec((1, width), vec),
            pl.BlockSpec((1, 1, width), lambda b, i: (b, 0, 0)),
        ],
        out_specs=pl.BlockSpec((1, t_rows, width), lambda b, i: (b, tile(i), 0)),
        out_shape=jax.ShapeDtypeStruct((bsz, seq, width), F32),
        scratch_shapes=[
            pltpu.VMEM((t_rows + 2 * halo, width), F32),
            pltpu.VMEM((t_rows, width), F32),
            pltpu.VMEM((t_rows, width), F32),
            pltpu.VMEM((1, width), F32),
        ],
        compiler_params=_params("arbitrary", "arbitrary"),
        name="lru_scan_bwd" if reverse else "lru_scan_fwd",
    )(px, px, px, conv_w, conv_b.reshape(1, width), wa, ba.reshape(1, width), wx, bx.reshape(1, width),
      lam.reshape(1, width), h0)


def _mixer_kernel(x_ref, gate_ref, u_ref, v_ref, yf_ref, yb_ref, sw_ref, sb_ref, wout_ref, g1_ref, o_ref,
                  ycat_ref):
    t_rows = x_ref.shape[1]
    lru_width = gate_ref.shape[2]
    groups, group_dim = sw_ref.shape[0], u_ref.shape[2] // sw_ref.shape[0]

    ycat_ref[:, 0:lru_width] = (_gelu(gate_ref[0]) * (yf_ref[0] + yb_ref[0])).astype(BF16)
    for n in range(t_rows // CHUNK):
        rows = slice(n * CHUNK, (n + 1) * CHUNK)
        for g in range(groups):
            cols = slice(g * group_dim, (g + 1) * group_dim)
            vg = _gelu(v_ref[0, rows, cols])
            mu = jnp.mean(vg, axis=-1, keepdims=True)
            dv = vg - mu
            var = jnp.mean(dv * dv, axis=-1, keepdims=True)
            vn = dv * lax.rsqrt(var + EPS)
            mixed = jnp.dot(sw_ref[g], vn.astype(BF16), preferred_element_type=F32) + sb_ref[g]
            ycat_ref[rows, lru_width + g * group_dim:lru_width + (g + 1) * group_dim] = (
                _gelu(u_ref[0, rows, cols]) * mixed).astype(BF16)
    y = jnp.dot(ycat_ref[...], wout_ref[...], preferred_element_type=F32)
    o_ref[0] = x_ref[0] + g1_ref[0] * y


def _mixer(x, proj, y_fwd, y_bwd, sgu_w, sgu_b_bcast, w_out, gate1, t_rows):
    bsz, seq, d = x.shape
    width = y_fwd.shape[2]
    tile3 = lambda c: pl.BlockSpec((1, t_rows, width), lambda b, i: (b, i, c))
    return pl.pallas_call(
        _mixer_kernel,
        grid=(bsz, seq // t_rows),
        in_specs=[
            pl.BlockSpec((1, t_rows, d), lambda b, i: (b, i, 0)),
            tile3(1), tile3(2), tile3(3),
            tile3(0), tile3(0),
            pl.BlockSpec(sgu_w.shape, lambda b, i: (0, 0, 0)),
            pl.BlockSpec(sgu_b_bcast.shape, lambda b, i: (0, 0, 0)),
            pl.BlockSpec(w_out.shape, lambda b, i: (0, 0)),
            pl.BlockSpec((1, 1, d), lambda b, i: (b, 0, 0)),
        ],
        out_specs=pl.BlockSpec((1, t_rows, d), lambda b, i: (b, i, 0)),
        out_shape=jax.ShapeDtypeStruct((bsz, seq, d), F32),
        scratch_shapes=[pltpu.VMEM((t_rows, w_out.shape[0]), BF16)],
        compiler_params=_params("arbitrary", "arbitrary"),
        name="mixer",
    )(x, proj, proj, proj, y_fwd, y_bwd, sgu_w, sgu_b_bcast, w_out, gate1)


def _oddeven_mergesort_pairs(n):
    pairs = []
    p = 1
    while p < n:
        k = p
        while k >= 1:
            for j in range(k % p, n - k, 2 * k):
                for i in range(min(k, n - j - k)):
                    if (i + j) // (2 * p) == (i + j + k) // (2 * p):
                        pairs.append((i + j, i + j + k))
            k //= 2
        p *= 2
    return pairs


def _bitonic_merge_pairs(n):
    pairs = []
    d = n // 2
    while d >= 1:
        pairs.extend((i, i + d) for i in range(n) if not i & d)
        d //= 2
    return pairs


def _compare_exchange(v, pairs):
    for i, j in pairs:
        a, b = v[i], v[j]
        if b is None:
            continue
        if a is None:
            v[i], v[j] = b, None
        else:
            v[i], v[j] = jnp.maximum(a, b), jnp.minimum(a, b)


def _top16_groups(groups):
    v = list(groups)
    n = len(v)
    _compare_exchange(v, _oddeven_mergesort_pairs(n))
    shift = V7X_SUBLANES // 2
    while shift >= 1:
        merged = []
        for i in range(n):
            a, b = v[i], v[n - 1 - i]
            b = None if b is None else pltpu.roll(b, shift, 0)
            merged.append(b if a is None else (a if b is None else jnp.maximum(a, b)))
        v = merged
        _compare_exchange(v, _bitonic_merge_pairs(n))
        shift //= 2
    return v


def _sublane_gather(groups, sub):
    out = groups[-1]
    for j in range(len(groups) - 2, -1, -1):
        out = jnp.where(sub == j, groups[j], out)
    return out


def _pair_candidates(v1, v2, sub):
    ns = V7X_SUBLANES
    neg_inf = jnp.float32(-jnp.inf)
    v2_lo = _sublane_gather(v2[:ns], sub)
    v2_hi = _sublane_gather(v2[ns:], sub)
    v1_hi = _sublane_gather(v1[ns:], sub)
    groups = [v1[0] + v2_lo, v1[0] + v2_hi, v1[1] + v2_lo]
    for i in range(2, ns):
        groups.append(jnp.where(sub < PEER_TOPK // (i + 1), v1[i] + v2_lo, neg_inf))
    groups.append(v1_hi + v2[0])
    return groups


def _bf16_pair_words(x):
    bits = pltpu.bitcast(x.astype(BF16).astype(F32), U32)
    return bits | lax.shift_right_logical(bits, jnp.uint32(16))


def _peer_route_kernel(x_ref, g_ref, shift_ref, scale_ref, wqt_ref, k1_ref, k2_ref,
                       ht_ref, rank2_ref, p2_ref, cntw_ref, p1w_ref):
    heads, n_keys, half = k1_ref.shape
    n_tok = x_ref.shape[0]
    ns = V7X_SUBLANES
    n_groups = n_keys // ns
    hn = _norm_modulate(x_ref[...], g_ref[...], shift_ref[0], scale_ref[0])
    ht = hn.T.astype(BF16)
    ht_ref[...] = ht
    qt = jnp.dot(wqt_ref[...], ht, preferred_element_type=F32)
    sub = lax.broadcasted_iota(jnp.int32, (ns, n_tok), 0)
    for h in range(heads):
        q1 = qt[(2 * h) * half:(2 * h + 1) * half].astype(BF16)
        q2 = qt[(2 * h + 1) * half:(2 * h + 2) * half].astype(BF16)
        s1 = jnp.dot(k1_ref[h], q1, preferred_element_type=F32)
        s2 = jnp.dot(k2_ref[h], q2, preferred_element_type=F32)
        s1g = s1.reshape(n_groups, ns, n_tok)
        s2g = s2.reshape(n_groups, ns, n_tok)
        v1 = _top16_groups([s1g[i] for i in range(n_groups)])
        v2 = _top16_groups([s2g[i] for i in range(n_groups)])
        cand = _pair_candidates(v1, v2, sub)
        tau = _top16_groups(cand + [None] * (n_groups - len(cand)))[PEER_TOPK - 1]
        m = v1[0] + v2[0]
        zs = None
        for cg in cand:
            term = jnp.where(cg >= tau, jnp.exp(cg - m), 0.0)
            zs = term if zs is None else zs + term
        inv_z = 1.0 / jnp.sum(zs, axis=0, keepdims=True)
        rank2 = jnp.zeros_like(s2g)
        cnt = jnp.zeros_like(s1g)
        for j in range(PEER_TOPK):
            rank2 = rank2 + jnp.where(v2[j][None] > s2g, 1.0, 0.0)
            cnt = cnt + jnp.where(s1g + v2[j][None] >= tau[None], 1.0, 0.0)
        rank2_ref[h] = rank2.reshape(n_keys, n_tok).astype(BF16)
        p2_ref[h] = jnp.exp(s2 - v2[0][0:1]).astype(BF16)
        cntw_ref[h] = _bf16_pair_words(cnt.reshape(n_keys, n_tok))
        p1w_ref[h] = _bf16_pair_words(jnp.exp(s1 - v1[0][0:1]) * inv_z)


def _peer_route(x2, g, shift, scale, wq_t, k1, k2, rows_per_vec, tm):
    n, d = x2.shape
    heads, n_keys, _ = k1.shape
    tiles_per_vec = rows_per_vec // tm
    tok_block = pl.BlockSpec((heads, n_keys, tm), lambda i: (0, 0, i))
    return pl.pallas_call(
        _peer_route_kernel,
        grid=(n // tm,),
        in_specs=[
            pl.BlockSpec((tm, d), lambda i: (i, 0)),
            pl.BlockSpec((1, d), lambda i: (0, 0)),
            pl.BlockSpec((1, 1, d), lambda i: (i // tiles_per_vec, 0, 0)),
            pl.BlockSpec((1, 1, d), lambda i: (i // tiles_per_vec, 0, 0)),
            pl.BlockSpec(wq_t.shape, lambda i: (0, 0)),
            pl.BlockSpec(k1.shape, lambda i: (0, 0, 0)),
            pl.BlockSpec(k2.shape, lambda i: (0, 0, 0)),
        ],
        out_specs=[pl.BlockSpec((d, tm), lambda i: (0, i)), tok_block, tok_block, tok_block, tok_block],
        out_shape=[
            jax.ShapeDtypeStruct((d, n), BF16),
            jax.ShapeDtypeStruct((heads, n_keys, n), BF16),
            jax.ShapeDtypeStruct((heads, n_keys, n), BF16),
            jax.ShapeDtypeStruct((heads, n_keys, n), U32),
            jax.ShapeDtypeStruct((heads, n_keys, n), U32),
        ],
        compiler_params=_params("arbitrary"),
        name="peer_route",
    )(x2, g, shift, scale, wq_t, k1, k2)


def _peer_experts_kernel(ht_ref, rank2_ref, p2_ref, cntw_ref, p1w_ref, u_ref, vt_ref, o_ref, acc_ref):
    e = pl.program_id(1)
    heads, n_keys, n_tok = rank2_ref.shape
    e1_per_block = u_ref.shape[0] // n_keys
    ts = V7X_MXU_COLS
    nb = V7X_BF16_ROWS
    ns = V7X_SUBLANES

    @pl.when(e == 0)
    def _():
        acc_ref[...] = jnp.zeros_like(acc_ref)

    n_exp_sub = 2
    e1_per_sub = e1_per_block // n_exp_sub
    eb_sub = e1_per_sub * n_keys

    def pre_activation(blk):
        cols, xs = blk
        return jnp.dot(u_ref[xs * eb_sub:(xs + 1) * eb_sub, :], ht_ref[:, cols],
                       preferred_element_type=F32)

    def weighted_activation(a_t, blk):
        cols, xs = blk
        blocks = []
        for j in range(e1_per_sub):
            e1 = e * e1_per_block + xs * e1_per_sub + j
            cnt = [pltpu.bitcast(jnp.broadcast_to(cntw_ref[h, pl.ds(e1, 1), cols], (ns, ts)), BF16)
                   for h in range(heads)]
            p1 = [pltpu.bitcast(jnp.broadcast_to(p1w_ref[h, pl.ds(e1, 1), cols], (ns, ts)), BF16)
                  for h in range(heads)]
            for rg in range(n_keys // nb):
                rows = slice(rg * nb, (rg + 1) * nb)
                w = None
                for h in range(heads):
                    wh = jnp.where(rank2_ref[h, rows, cols] < cnt[h], p2_ref[h, rows, cols] * p1[h],
                                   jnp.zeros((nb, ts), BF16))
                    w = wh if w is None else w + wh
                act = _gelu(a_t[j * n_keys + rg * nb:j * n_keys + (rg + 1) * nb])
                blocks.append(w * act.astype(BF16))
        return jnp.concatenate(blocks, axis=0)

    def accumulate(wa, blk):
        cols, xs = blk
        acc_ref[:, cols] += jnp.dot(vt_ref[:, xs * eb_sub:(xs + 1) * eb_sub], wa,
                                    preferred_element_type=F32)

    chains = [(slice(s * ts, (s + 1) * ts), xs) for s in range(n_tok // ts) for xs in range(n_exp_sub)]
    n_chain = len(chains)
    pre = {0: pre_activation(chains[0])}
    wa = {}
    for k in range(n_chain + 1):
        if k + 1 < n_chain:
            pre[k + 1] = pre_activation(chains[k + 1])
        if k >= 1:
            accumulate(wa.pop(k - 1), chains[k - 1])
        if k < n_chain:
            wa[k] = weighted_activation(pre.pop(k), chains[k])

    @pl.when(e == pl.num_programs(1) - 1)
    def _():
        o_ref[...] = acc_ref[...].T


def _peer_experts(ht, rank2, p2, cntw, p1w, u_bf, vt_bf, tm, eb):
    d, n = ht.shape
    heads, n_keys, _ = rank2.shape
    n_exp = u_bf.shape[0]
    tok_block = pl.BlockSpec((heads, n_keys, tm), lambda i, e: (0, 0, i))
    return pl.pallas_call(
        _peer_experts_kernel,
        grid=(n // tm, n_exp // eb),
        in_specs=[
            pl.BlockSpec((d, tm), lambda i, e: (0, i)),
            tok_block, tok_block, tok_block, tok_block,
            pl.BlockSpec((eb, d), lambda i, e: (e, 0)),
            pl.BlockSpec((d, eb), lambda i, e: (0, e)),
        ],
        out_specs=pl.BlockSpec((tm, d), lambda i, e: (i, 0)),
        out_shape=jax.ShapeDtypeStruct((n, d), F32),
        scratch_shapes=[pltpu.VMEM((d, tm), F32)],
        compiler_params=_params("arbitrary", "arbitrary"),
        name="peer_experts",
    )(ht, rank2, p2, cntw, p1w, u_bf, vt_bf)


def _final_kernel(x_ref, p_ref, gate_ref, g_ref, o_ref):
    x = x_ref[...] + gate_ref[0] * p_ref[...]
    ms = jnp.mean(x * x, axis=-1, keepdims=True)
    o_ref[...] = x * lax.rsqrt(ms + EPS) * g_ref[...]


def _final(x2, peer, gate, g, rows_per_vec, tm):
    n, d = x2.shape
    tiles_per_vec = rows_per_vec // tm
    return pl.pallas_call(
        _final_kernel,
        grid=(n // tm,),
        in_specs=[
            pl.BlockSpec((tm, d), lambda i: (i, 0)),
            pl.BlockSpec((tm, d), lambda i: (i, 0)),
            pl.BlockSpec((1, 1, d), lambda i: (i // tiles_per_vec, 0, 0)),
            pl.BlockSpec((1, d), lambda i: (0, 0)),
        ],
        out_specs=pl.BlockSpec((tm, d), lambda i: (i, 0)),
        out_shape=jax.ShapeDtypeStruct((n, d), F32),
        compiler_params=_params("arbitrary"),
        name="final_norm",
    )(x2, peer, gate, g)


def kernel(x, c, ctx, c_ctx, w_mod, b_mod, norm1_g, norm2_g, w_in, conv_w, conv_b, lru_wa, lru_ba, lru_wx,
           lru_bx, lru_lambda, sgu_w, sgu_b, w_out, peer_wq, peer_k1, peer_k2, peer_u, peer_v, final_g):
    bsz, seq, d = x.shape
    ctx_len = ctx.shape[1]
    depth = w_mod.shape[0]
    assert depth == 1, "only the single-layer configuration of the reference is implemented"
    lru_width = conv_w.shape[2]
    assert seq % 512 == 0 and ctx_len % CHUNK == 0 and d % V7X_LANES == 0

    t_seq = 512
    t_tok = 512
    expert_block = 1024

    w_in_bf = w_in[0].astype(BF16)
    w_out_bf = w_out[0].astype(BF16)
    wa_bf, wx_bf = lru_wa[0].astype(BF16), lru_wx[0].astype(BF16)
    sgu_w_bf = sgu_w[0].astype(BF16)
    sgu_b_bcast = jnp.broadcast_to(sgu_b[0][:, :, None], sgu_b.shape[1:] + (sgu_w.shape[-1],))
    wq_t_bf = peer_wq[0].T.astype(BF16)
    k1_bf, k2_bf = peer_k1[0].astype(BF16), peer_k2[0].astype(BF16)
    u_bf = peer_u[0].astype(BF16)
    vt_bf = peer_v[0].T.astype(BF16)

    mod = _modulation(jnp.concatenate([c, c_ctx[None, :]], axis=0), w_mod[0], b_mod[0])
    mod = mod.reshape(bsz + 1, N_MOD, 1, d)
    lat = lambda k: mod[:bsz, k]
    ctx_vec = lambda k: mod[bsz:, k]
    g1 = norm1_g[0].reshape(1, d)
    g2 = norm2_g[0].reshape(1, d)

    px_ctx = _norm_proj(ctx.reshape(bsz * ctx_len, d), g1, ctx_vec(0), ctx_vec(1), w_in_bf[:, :lru_width],
                        rows_per_vec=bsz * ctx_len, tm=ctx_len).reshape(bsz, ctx_len, lru_width)
    proj = _norm_proj(x.reshape(bsz * seq, d), g1, lat(0), lat(1), w_in_bf,
                      rows_per_vec=seq, tm=1024).reshape(bsz, seq, w_in.shape[2])

    zeros_h = jnp.zeros((bsz, 1, lru_width), F32)
    ys = []
    for direction, reverse in enumerate((False, True)):
        prm = (conv_w[0], conv_b[0], wa_bf[direction], lru_ba[0, direction], wx_bf[direction],
               lru_bx[0, direction], lru_lambda[0, direction])
        y_ctx = _lru_scan(px_ctx, *prm, zeros_h, reverse, ctx_len)
        h_ctx = y_ctx[:, 0:1] if reverse else y_ctx[:, ctx_len - 1:ctx_len]
        ys.append(_lru_scan(proj, *prm, h_ctx, reverse, t_seq))

    x_mid = _mixer(x, proj, ys[0], ys[1], sgu_w_bf, sgu_b_bcast, w_out_bf, lat(2), t_seq)

    x2 = x_mid.reshape(bsz * seq, d)
    ht, rank2, p2, cntw, p1w = _peer_route(x2, g2, lat(3), lat(4), wq_t_bf, k1_bf, k2_bf,
                                           rows_per_vec=seq, tm=t_tok)
    peer = _peer_experts(ht, rank2, p2, cntw, p1w, u_bf, vt_bf, t_tok, expert_block)
    out = _final(x2, peer, lat(5), final_g.reshape(1, d), rows_per_vec=seq, tm=512)
    return out.reshape(bsz, seq, d)
```

```python
import functools

import jax
import jax.numpy as jnp
from jax import lax
from jax.experimental import pallas as pl
from jax.experimental.pallas import tpu as pltpu

F32 = jnp.float32
BF16 = jnp.bfloat16
U32 = jnp.uint32

EPS = 1e-6
LRU_C = 8.0
CONV_LEFT = 2
CHUNK = 128
PEER_TOPK = 16
N_MOD = 6

V7X_LANES = 128
V7X_SUBLANES = 8
V7X_BF16_ROWS = 16
V7X_MXU_COLS = 256
V7X_VMEM_LIMIT_BYTES = 56 * 1024 * 1024


def _params(*semantics):
    return pltpu.CompilerParams(dimension_semantics=semantics, vmem_limit_bytes=V7X_VMEM_LIMIT_BYTES)


def _gelu(x):
    return 0.5 * x * (1.0 + jnp.tanh(0.7978845608028654 * (x + 0.044715 * (x * x * x))))


def _norm_modulate(x, g, shift, scale):
    ms = jnp.mean(x * x, axis=-1, keepdims=True)
    return (x * lax.rsqrt(ms + EPS) * g) * (1.0 + scale) + shift


def _modulation_kernel(cs_ref, w_ref, b_ref, o_ref):
    n_rows = cs_ref.shape[0]
    nb = w_ref.shape[1]
    rows = []
    for r in range(n_rows):
        c = cs_ref[r]
        s = c * jax.nn.sigmoid(c)
        parts = [jnp.sum(w_ref[:, n * V7X_LANES:(n + 1) * V7X_LANES] * s, axis=0, keepdims=True)
                 for n in range(nb // V7X_LANES)]
        rows.append(jnp.concatenate(parts, axis=1) + b_ref[...])
    rows.append(jnp.zeros((V7X_SUBLANES - n_rows, nb), F32))
    o_ref[...] = jnp.concatenate(rows, axis=0)


def _modulation(cs, w_mod, b_mod):
    n_rows, d = cs.shape
    n = w_mod.shape[1]
    nb = 1024
    csb = jnp.broadcast_to(cs[:, :, None], (n_rows, d, V7X_LANES))
    out = pl.pallas_call(
        _modulation_kernel,
        grid=(n // nb,),
        in_specs=[
            pl.BlockSpec((n_rows, d, V7X_LANES), lambda j: (0, 0, 0)),
            pl.BlockSpec((d, nb), lambda j: (0, j)),
            pl.BlockSpec((1, nb), lambda j: (0, j)),
        ],
        out_specs=pl.BlockSpec((V7X_SUBLANES, nb), lambda j: (0, j)),
        out_shape=jax.ShapeDtypeStruct((V7X_SUBLANES, n), F32),
        compiler_params=_params("arbitrary"),
        name="modulation",
    )(csb, w_mod, b_mod.reshape(1, n))
    return out[:n_rows]


def _norm_proj_kernel(x_ref, g_ref, shift_ref, scale_ref, w_ref, o_ref):
    hn = _norm_modulate(x_ref[...], g_ref[...], shift_ref[0], scale_ref[0])
    o_ref[...] = jnp.dot(hn.astype(BF16), w_ref[...], preferred_element_type=F32)


def _norm_proj(x2, g, shift, scale, w, rows_per_vec, tm):
    r, d = x2.shape
    n = w.shape[1]
    tn = 1024
    tiles_per_vec = rows_per_vec // tm
    return pl.pallas_call(
        _norm_proj_kernel,
        grid=(r // tm, n // tn),
        in_specs=[
            pl.BlockSpec((tm, d), lambda i, j: (i, 0)),
            pl.BlockSpec((1, d), lambda i, j: (0, 0)),
            pl.BlockSpec((1, 1, d), lambda i, j: (i // tiles_per_vec, 0, 0)),
            pl.BlockSpec((1, 1, d), lambda i, j: (i // tiles_per_vec, 0, 0)),
            pl.BlockSpec((d, tn), lambda i, j: (0, j)),
        ],
        out_specs=pl.BlockSpec((tm, tn), lambda i, j: (i, j)),
        out_shape=jax.ShapeDtypeStruct((r, n), F32),
        compiler_params=_params("arbitrary", "arbitrary"),
        name="norm_proj",
    )(x2, g, shift, scale, w)


def _lru_scan_kernel(cur_ref, prev_ref, next_ref, cw_ref, cb_ref, wa_ref, ba_ref, wx_ref, bx_ref,
                     lam_ref, h0_ref, y_ref, xe_ref, a_ref, b_ref, h_ref, *, reverse, n_tiles):
    i = pl.program_id(1)
    ti = (n_tiles - 1 - i) if reverse else i
    t_rows, width = a_ref.shape
    heads, head_dim = wa_ref.shape[0], wa_ref.shape[1]
    halo = V7X_SUBLANES

    @pl.when(i == 0)
    def _():
        h_ref[...] = h0_ref[0]

    zero = jnp.zeros((halo, width), F32)
    xe_ref[0:halo, :] = jnp.where(ti == 0, zero, prev_ref[0])
    xe_ref[halo:halo + t_rows, :] = cur_ref[0]
    xe_ref[halo + t_rows:, :] = jnp.where(ti == n_tiles - 1, zero, next_ref[0])

    xc = jnp.broadcast_to(cb_ref[...], (t_rows, width))
    for k in range(cw_ref.shape[0]):
        lo = halo - CONV_LEFT + k
        xc = xc + xe_ref[lo:lo + t_rows, :] * cw_ref[k:k + 1, :]

    xcb = xc.astype(BF16)
    r_parts, i_parts = [], []
    for h in range(heads):
        xh = xcb[:, h * head_dim:(h + 1) * head_dim]
        r_parts.append(jnp.dot(xh, wa_ref[h], preferred_element_type=F32))
        i_parts.append(jnp.dot(xh, wx_ref[h], preferred_element_type=F32))
    r = jax.nn.sigmoid(jnp.concatenate(r_parts, axis=1) + ba_ref[...])
    ig = jax.nn.sigmoid(jnp.concatenate(i_parts, axis=1) + bx_ref[...])
    lam = lam_ref[...]
    softplus_neg_lam = jnp.maximum(-lam, 0.0) + jnp.log1p(jnp.exp(-jnp.abs(lam)))
    log_a = -LRU_C * r * softplus_neg_lam
    a_ref[...] = jnp.exp(log_a)
    th = jnp.tanh(log_a)
    b_ref[...] = jnp.sqrt(-2.0 * th / (1.0 - th)) * (ig * xc)

    def step(s, h):
        t = (t_rows - 1 - s) if reverse else s
        h = a_ref[pl.ds(t, 1), :] * h + b_ref[pl.ds(t, 1), :]
        y_ref[0, pl.ds(t, 1), :] = h
        return h

    h_ref[...] = lax.fori_loop(0, t_rows, step, h_ref[...], unroll=8)


def _lru_scan(px, conv_w, conv_b, wa, ba, wx, bx, lam, h0, reverse, t_rows):
    bsz, seq, _ = px.shape
    width = conv_w.shape[1]
    n_tiles = seq // t_rows
    halo = V7X_SUBLANES
    tb = t_rows // halo
    last_halo_block = seq // halo - 1

    def tile(i):
        return (n_tiles - 1 - i) if reverse else i

    vec = lambda b, i: (0, 0)
    kern = functools.partial(_lru_scan_kernel, reverse=reverse, n_tiles=n_tiles)
    return pl.pallas_call(
        kern,
        grid=(bsz, n_tiles),
        in_specs=[
            pl.BlockSpec((1, t_rows, width), lambda b, i: (b, tile(i), 0)),
            pl.BlockSpec((1, halo, width), lambda b, i: (b, jnp.maximum(tile(i) * tb - 1, 0), 0)),
            pl.BlockSpec((1, halo, width), lambda b, i: (b, jnp.minimum((tile(i) + 1) * tb, last_halo_block), 0)),
            pl.BlockSpec(conv_w.shape, vec),
            pl.BlockSpec((1, width), vec),
            pl.BlockSpec(wa.shape, lambda b, i: (0, 0, 0)),
            pl.BlockSpec((1, width), vec),
            pl.BlockSpec(wx.shape, lambda b, i: (0, 0, 0)),
            pl.BlockSpec((1, width), vec),
            pl.BlockSpec((1, width), vec),
            pl.BlockSpec((1, 1, width), lambda b, i: (b, 0, 0)),
        ],
        out_specs=pl.BlockSpec((1, t_rows, width), lambda b, i: (b, tile(i), 0)),
        out_shape=jax.ShapeDtypeStruct((bsz, seq, width), F32),
        scratch_shapes=[
            pltpu.VMEM((t_rows + 2 * halo, width), F32),
            pltpu.VMEM((t_rows, width), F32),
            pltpu.VMEM((t_rows, width), F32),
            pltpu.VMEM((1, width), F32),
        ],
        compiler_params=_params("arbitrary", "arbitrary"),
        name="lru_scan_bwd" if reverse else "lru_scan_fwd",
    )(px, px, px, conv_w, conv_b.reshape(1, width), wa, ba.reshape(1, width), wx, bx.reshape(1, width),
      lam.reshape(1, width), h0)


def _mixer_kernel(x_ref, gate_ref, u_ref, v_ref, yf_ref, yb_ref, sw_ref, sb_ref, wout_ref, g1_ref, o_ref,
                  ycat_ref):
    t_rows = x_ref.shape[1]
    lru_width = gate_ref.shape[2]
    groups, group_dim = sw_ref.shape[0], u_ref.shape[2] // sw_ref.shape[0]

    ycat_ref[:, 0:lru_width] = (_gelu(gate_ref[0]) * (yf_ref[0] + yb_ref[0])).astype(BF16)
    for n in range(t_rows // CHUNK):
        rows = slice(n * CHUNK, (n + 1) * CHUNK)
        for g in range(groups):
            cols = slice(g * group_dim, (g + 1) * group_dim)
            vg = _gelu(v_ref[0, rows, cols])
            mu = jnp.mean(vg, axis=-1, keepdims=True)
            dv = vg - mu
            var = jnp.mean(dv * dv, axis=-1, keepdims=True)
            vn = dv * lax.rsqrt(var + EPS)
            mixed = jnp.dot(sw_ref[g], vn.astype(BF16), preferred_element_type=F32) + sb_ref[g]
            ycat_ref[rows, lru_width + g * group_dim:lru_width + (g + 1) * group_dim] = (
                _gelu(u_ref[0, rows, cols]) * mixed).astype(BF16)
    y = jnp.dot(ycat_ref[...], wout_ref[...], preferred_element_type=F32)
    o_ref[0] = x_ref[0] + g1_ref[0] * y


def _mixer(x, proj, y_fwd, y_bwd, sgu_w, sgu_b_bcast, w_out, gate1, t_rows):
    bsz, seq, d = x.shape
    width = y_fwd.shape[2]
    tile3 = lambda c: pl.BlockSpec((1, t_rows, width), lambda b, i: (b, i, c))
    return pl.pallas_call(
        _mixer_kernel,
        grid=(bsz, seq // t_rows),
        in_specs=[
            pl.BlockSpec((1, t_rows, d), lambda b, i: (b, i, 0)),
            tile3(1), tile3(2), tile3(3),
            tile3(0), tile3(0),
            pl.BlockSpec(sgu_w.shape, lambda b, i: (0, 0, 0)),
            pl.BlockSpec(sgu_b_bcast.shape, lambda b, i: (0, 0, 0)),
            pl.BlockSpec(w_out.shape, lambda b, i: (0, 0)),
            pl.BlockSpec((1, 1, d), lambda b, i: (b, 0, 0)),
        ],
        out_specs=pl.BlockSpec((1, t_rows, d), lambda b, i: (b, i, 0)),
        out_shape=jax.ShapeDtypeStruct((bsz, seq, d), F32),
        scratch_shapes=[pltpu.VMEM((t_rows, w_out.shape[0]), BF16)],
        compiler_params=_params("arbitrary", "arbitrary"),
        name="mixer",
    )(x, proj, proj, proj, y_fwd, y_bwd, sgu_w, sgu_b_bcast, w_out, gate1)


def _oddeven_mergesort_pairs(n):
    pairs = []
    p = 1
    while p < n:
        k = p
        while k >= 1:
            for j in range(k % p, n - k, 2 * k):
                for i in range(min(k, n - j - k)):
                    if (i + j) // (2 * p) == (i + j + k) // (2 * p):
                        pairs.append((i + j, i + j + k))
            k //= 2
        p *= 2
    return pairs


def _bitonic_merge_pairs(n):
    pairs = []
    d = n // 2
    while d >= 1:
        pairs.extend((i, i + d) for i in range(n) if not i & d)
        d //= 2
    return pairs


def _compare_exchange(v, pairs):
    for i, j in pairs:
        a, b = v[i], v[j]
        if b is None:
            continue
        if a is None:
            v[i], v[j] = b, None
        else:
            v[i], v[j] = jnp.maximum(a, b), jnp.minimum(a, b)


def _top16_groups(groups):
    v = list(groups)
    n = len(v)
    _compare_exchange(v, _oddeven_mergesort_pairs(n))
    shift = V7X_SUBLANES // 2
    while shift >= 1:
        merged = []
        for i in range(n):
            a, b = v[i], v[n - 1 - i]
            b = None if b is None else pltpu.roll(b, shift, 0)
            merged.append(b if a is None else (a if b is None else jnp.maximum(a, b)))
        v = merged
        _compare_exchange(v, _bitonic_merge_pairs(n))
        shift //= 2
    return v


def _sublane_gather(groups, sub):
    out = groups[-1]
    for j in range(len(groups) - 2, -1, -1):
        out = jnp.where(sub == j, groups[j], out)
    return out


def _pair_candidates(v1, v2, sub):
    ns = V7X_SUBLANES
    neg_inf = jnp.float32(-jnp.inf)
    v2_lo = _sublane_gather(v2[:ns], sub)
    v2_hi = _sublane_gather(v2[ns:], sub)
    v1_hi = _sublane_gather(v1[ns:], sub)
    groups = [v1[0] + v2_lo, v1[0] + v2_hi, v1[1] + v2_lo]
    for i in range(2, ns):
        groups.append(jnp.where(sub < PEER_TOPK // (i + 1), v1[i] + v2_lo, neg_inf))
    groups.append(v1_hi + v2[0])
    return groups


def _monotone_count(v, pred):
    w = jnp.where
    c1 = pred(v[7])
    c2 = pred(w(c1, v[11], v[3]))
    c3 = pred(w(c1, w(c2, v[13], v[9]), w(c2, v[5], v[1])))
    c4 = pred(w(c1, w(c2, w(c3, v[14], v[12]), w(c3, v[10], v[8])),
                w(c2, w(c3, v[6], v[4]), w(c3, v[2], v[0]))))
    c5 = pred(v[15])
    return (w(c1, 8.0, 0.0) + w(c2, 4.0, 0.0)) + (w(c3, 2.0, 0.0) + w(c4, 1.0, 0.0)) + w(c5, 1.0, 0.0)


def _bf16_pair_words(x):
    bits = pltpu.bitcast(x.astype(BF16).astype(F32), U32)
    return bits | lax.shift_right_logical(bits, jnp.uint32(16))


def _peer_route_kernel(x_ref, g_ref, shift_ref, scale_ref, wqt_ref, k1_ref, k2_ref,
                       ht_ref, rank2_ref, p2_ref, cntw_ref, p1w_ref):
    heads, n_keys, half = k1_ref.shape
    n_tok = x_ref.shape[0]
    ns = V7X_SUBLANES
    n_groups = n_keys // ns
    hn = _norm_modulate(x_ref[...], g_ref[...], shift_ref[0], scale_ref[0])
    ht = hn.T.astype(BF16)
    ht_ref[...] = ht
    qt = jnp.dot(wqt_ref[...], ht, preferred_element_type=F32)
    lanes = V7X_LANES
    sub = lax.broadcasted_iota(jnp.int32, (ns, lanes), 0)
    for h in range(heads):
        q1 = qt[(2 * h) * half:(2 * h + 1) * half].astype(BF16)
        q2 = qt[(2 * h + 1) * half:(2 * h + 2) * half].astype(BF16)
        s1_all = jnp.dot(k1_ref[h], q1, preferred_element_type=F32)
        s2_all = jnp.dot(k2_ref[h], q2, preferred_element_type=F32)
        for c in range(n_tok // lanes):
            cols = slice(c * lanes, (c + 1) * lanes)
            s1, s2 = s1_all[:, cols], s2_all[:, cols]
            s1g = s1.reshape(n_groups, ns, lanes)
            s2g = s2.reshape(n_groups, ns, lanes)
            v1 = _top16_groups([s1g[i] for i in range(n_groups)])
            v2 = _top16_groups([s2g[i] for i in range(n_groups)])
            cand = _pair_candidates(v1, v2, sub)
            tau = _top16_groups(cand + [None] * (n_groups - len(cand)))[PEER_TOPK - 1]
            m = v1[0] + v2[0]
            zs = None
            for cg in cand:
                term = jnp.where(cg >= tau, jnp.exp(cg - m), 0.0)
                zs = term if zs is None else zs + term
            inv_z = 1.0 / jnp.sum(zs, axis=0, keepdims=True)
            v2b = [v[None] for v in v2]
            rank2 = _monotone_count(v2b, lambda t: t > s2g)
            cnt = _monotone_count(v2b, lambda t: s1g + t >= tau[None])
            rank2_ref[h, :, cols] = rank2.reshape(n_keys, lanes).astype(BF16)
            p2_ref[h, :, cols] = jnp.exp(s2 - v2[0][0:1]).astype(BF16)
            cntw_ref[h, :, cols] = _bf16_pair_words(cnt.reshape(n_keys, lanes))
            p1w_ref[h, :, cols] = _bf16_pair_words(jnp.exp(s1 - v1[0][0:1]) * inv_z)


def _peer_route(x2, g, shift, scale, wq_t, k1, k2, rows_per_vec, tm):
    n, d = x2.shape
    heads, n_keys, _ = k1.shape
    tiles_per_vec = rows_per_vec // tm
    tok_block = pl.BlockSpec((heads, n_keys, tm), lambda i: (0, 0, i))
    return pl.pallas_call(
        _peer_route_kernel,
        grid=(n // tm,),
        in_specs=[
            pl.BlockSpec((tm, d), lambda i: (i, 0)),
            pl.BlockSpec((1, d), lambda i: (0, 0)),
            pl.BlockSpec((1, 1, d), lambda i: (i // tiles_per_vec, 0, 0)),
            pl.BlockSpec((1, 1, d), lambda i: (i // tiles_per_vec, 0, 0)),
            pl.BlockSpec(wq_t.shape, lambda i: (0, 0)),
            pl.BlockSpec(k1.shape, lambda i: (0, 0, 0)),
            pl.BlockSpec(k2.shape, lambda i: (0, 0, 0)),
        ],
        out_specs=[pl.BlockSpec((d, tm), lambda i: (0, i)), tok_block, tok_block, tok_block, tok_block],
        out_shape=[
            jax.ShapeDtypeStruct((d, n), BF16),
            jax.ShapeDtypeStruct((heads, n_keys, n), BF16),
            jax.ShapeDtypeStruct((heads, n_keys, n), BF16),
            jax.ShapeDtypeStruct((heads, n_keys, n), U32),
            jax.ShapeDtypeStruct((heads, n_keys, n), U32),
        ],
        compiler_params=_params("arbitrary"),
        name="peer_route",
    )(x2, g, shift, scale, wq_t, k1, k2)


def _peer_experts_kernel(ht_ref, rank2_ref, p2_ref, cntw_ref, p1w_ref, u_ref, vt_ref, o_ref, wa_cur_ref, wa_next_ref,
                         handoff_sem, *, n_blocks, n_exp_blocks):
    g = pl.program_id(0)
    heads, n_keys, n_tok = rank2_ref.shape
    e1_per_block = u_ref.shape[0] // n_keys
    nb = V7X_BF16_ROWS
    ns = V7X_SUBLANES
    eb_act = lax.rem(jnp.minimum(g, n_blocks - 1), n_exp_blocks)
    eb_val = lax.rem(jnp.maximum(g - 1, 0), n_exp_blocks)

    @pl.when(g == 0)
    def _():
        wa_next_ref[...] = jnp.zeros_like(wa_next_ref)

    @pl.when(eb_val == 0)
    def _():
        o_ref[...] = jnp.zeros_like(o_ref)

    handoff = pltpu.make_async_copy(wa_next_ref, wa_cur_ref, handoff_sem)
    handoff.start()

    n_exp_sub = 2
    e1_per_sub = e1_per_block // n_exp_sub
    eb_sub = e1_per_sub * n_keys

    def weighted_activation(a_t, xs):
        blocks = []
        for j in range(e1_per_sub):
            e1 = eb_act * e1_per_block + xs * e1_per_sub + j
            cnt = [pltpu.bitcast(jnp.broadcast_to(cntw_ref[h, pl.ds(e1, 1), :], (ns, n_tok)), BF16)
                   for h in range(heads)]
            p1 = [pltpu.bitcast(jnp.broadcast_to(p1w_ref[h, pl.ds(e1, 1), :], (ns, n_tok)), BF16)
                  for h in range(heads)]
            for rg in range(n_keys // nb):
                rows = slice(rg * nb, (rg + 1) * nb)
                w = None
                for h in range(heads):
                    wh = jnp.where(rank2_ref[h, rows, :] < cnt[h], p2_ref[h, rows, :] * p1[h],
                                   jnp.zeros((nb, n_tok), BF16))
                    w = wh if w is None else w + wh
                act = _gelu(a_t[j * n_keys + rg * nb:j * n_keys + (rg + 1) * nb])
                blocks.append(w * act.astype(BF16))
        return jnp.concatenate(blocks, axis=0)

    pre = [jnp.dot(u_ref[xs * eb_sub:(xs + 1) * eb_sub, :], ht_ref[...], preferred_element_type=F32)
           for xs in range(n_exp_sub)]
    handoff.wait()
    o_ref[...] += jnp.dot(vt_ref[...], wa_cur_ref[...], preferred_element_type=F32)
    for xs in range(n_exp_sub):
        wa_next_ref[xs * eb_sub:(xs + 1) * eb_sub, :] = weighted_activation(pre[xs], xs)


def _peer_experts(ht, rank2, p2, cntw, p1w, u_bf, vt_bf, tm, eb):
    d, n = ht.shape
    heads, n_keys, _ = rank2.shape
    n_exp_blocks = u_bf.shape[0] // eb
    n_blocks = (n // tm) * n_exp_blocks

    def act_block(g):
        return jnp.minimum(g, n_blocks - 1)

    def val_block(g):
        return jnp.maximum(g - 1, 0)

    tok_block = pl.BlockSpec((heads, n_keys, tm), lambda g: (0, 0, act_block(g) // n_exp_blocks))
    kern = functools.partial(_peer_experts_kernel, n_blocks=n_blocks, n_exp_blocks=n_exp_blocks)
    return pl.pallas_call(
        kern,
        grid=(n_blocks + 1,),
        in_specs=[
            pl.BlockSpec((d, tm), lambda g: (0, act_block(g) // n_exp_blocks)),
            tok_block, tok_block, tok_block, tok_block,
            pl.BlockSpec((eb, d), lambda g: (act_block(g) % n_exp_blocks, 0)),
            pl.BlockSpec((d, eb), lambda g: (0, val_block(g) % n_exp_blocks)),
        ],
        out_specs=pl.BlockSpec((d, tm), lambda g: (0, val_block(g) // n_exp_blocks)),
        out_shape=jax.ShapeDtypeStruct((d, n), F32),
        scratch_shapes=[pltpu.VMEM((eb, tm), BF16), pltpu.VMEM((eb, tm), BF16), pltpu.SemaphoreType.DMA(())],
        compiler_params=_params("arbitrary"),
        name="peer_experts",
    )(ht, rank2, p2, cntw, p1w, u_bf, vt_bf)


def _final_kernel(x_ref, pt_ref, gate_ref, g_ref, o_ref):
    x = x_ref[...] + gate_ref[0] * pt_ref[...].T
    ms = jnp.mean(x * x, axis=-1, keepdims=True)
    o_ref[...] = x * lax.rsqrt(ms + EPS) * g_ref[...]


def _final(x2, peer_t, gate, g, rows_per_vec, tm):
    n, d = x2.shape
    tiles_per_vec = rows_per_vec // tm
    return pl.pallas_call(
        _final_kernel,
        grid=(n // tm,),
        in_specs=[
            pl.BlockSpec((tm, d), lambda i: (i, 0)),
            pl.BlockSpec((d, tm), lambda i: (0, i)),
            pl.BlockSpec((1, 1, d), lambda i: (i // tiles_per_vec, 0, 0)),
            pl.BlockSpec((1, d), lambda i: (0, 0)),
        ],
        out_specs=pl.BlockSpec((tm, d), lambda i: (i, 0)),
        out_shape=jax.ShapeDtypeStruct((n, d), F32),
        compiler_params=_params("arbitrary"),
        name="final_norm",
    )(x2, peer_t, gate, g)


def kernel(x, c, ctx, c_ctx, w_mod, b_mod, norm1_g, norm2_g, w_in, conv_w, conv_b, lru_wa, lru_ba, lru_wx,
           lru_bx, lru_lambda, sgu_w, sgu_b, w_out, peer_wq, peer_k1, peer_k2, peer_u, peer_v, final_g):
    bsz, seq, d = x.shape
    ctx_len = ctx.shape[1]
    depth = w_mod.shape[0]
    assert depth == 1, "only the single-layer configuration of the reference is implemented"
    lru_width = conv_w.shape[2]
    assert seq % 512 == 0 and ctx_len % CHUNK == 0 and d % V7X_LANES == 0

    t_seq = 512
    t_tok = 512
    expert_block = 1024

    w_in_bf = w_in[0].astype(BF16)
    w_out_bf = w_out[0].astype(BF16)
    wa_bf, wx_bf = lru_wa[0].astype(BF16), lru_wx[0].astype(BF16)
    sgu_w_bf = sgu_w[0].astype(BF16)
    sgu_b_bcast = jnp.broadcast_to(sgu_b[0][:, :, None], sgu_b.shape[1:] + (sgu_w.shape[-1],))
    wq_t_bf = peer_wq[0].T.astype(BF16)
    k1_bf, k2_bf = peer_k1[0].astype(BF16), peer_k2[0].astype(BF16)
    u_bf = peer_u[0].astype(BF16)
    vt_bf = peer_v[0].T.astype(BF16)

    mod = _modulation(jnp.concatenate([c, c_ctx[None, :]], axis=0), w_mod[0], b_mod[0])
    mod = mod.reshape(bsz + 1, N_MOD, 1, d)
    lat = lambda k: mod[:bsz, k]
    ctx_vec = lambda k: mod[bsz:, k]
    g1 = norm1_g[0].reshape(1, d)
    g2 = norm2_g[0].reshape(1, d)

    px_ctx = _norm_proj(ctx.reshape(bsz * ctx_len, d), g1, ctx_vec(0), ctx_vec(1), w_in_bf[:, :lru_width],
                        rows_per_vec=bsz * ctx_len, tm=ctx_len).reshape(bsz, ctx_len, lru_width)
    proj = _norm_proj(x.reshape(bsz * seq, d), g1, lat(0), lat(1), w_in_bf,
                      rows_per_vec=seq, tm=1024).reshape(bsz, seq, w_in.shape[2])

    zeros_h = jnp.zeros((bsz, 1, lru_width), F32)
    ys = []
    for direction, reverse in enumerate((False, True)):
        prm = (conv_w[0], conv_b[0], wa_bf[direction], lru_ba[0, direction], wx_bf[direction],
               lru_bx[0, direction], lru_lambda[0, direction])
        y_ctx = _lru_scan(px_ctx, *prm, zeros_h, reverse, ctx_len)
        h_ctx = y_ctx[:, 0:1] if reverse else y_ctx[:, ctx_len - 1:ctx_len]
        ys.append(_lru_scan(proj, *prm, h_ctx, reverse, t_seq))

    x_mid = _mixer(x, proj, ys[0], ys[1], sgu_w_bf, sgu_b_bcast, w_out_bf, lat(2), t_seq)

    x2 = x_mid.reshape(bsz * seq, d)
    ht, rank2, p2, cntw, p1w = _peer_route(x2, g2, lat(3), lat(4), wq_t_bf, k1_bf, k2_bf,
                                           rows_per_vec=seq, tm=t_tok)
    peer_t = _peer_experts(ht, rank2, p2, cntw, p1w, u_bf, vt_bf, t_tok, expert_block)
    out = _final(x2, peer_t, lat(5), final_g.reshape(1, d), rows_per_vec=seq, tm=512)
    return out.reshape(bsz, seq, d)
```

```python
import functools

import jax
import jax.numpy as jnp
from jax import lax
from jax.experimental import pallas as pl
from jax.experimental.pallas import tpu as pltpu

F32 = jnp.float32
BF16 = jnp.bfloat16
U32 = jnp.uint32

EPS = 1e-6
LRU_C = 8.0
CONV_LEFT = 2
CHUNK = 128
PEER_TOPK = 16
N_MOD = 6

V7X_LANES = 128
V7X_SUBLANES = 8
V7X_BF16_ROWS = 16
V7X_MXU_COLS = 256
V7X_VMEM_LIMIT_BYTES = 56 * 1024 * 1024


def _params(*semantics):
    return pltpu.CompilerParams(dimension_semantics=semantics, vmem_limit_bytes=V7X_VMEM_LIMIT_BYTES)


def _gelu(x):
    return 0.5 * x * (1.0 + jnp.tanh(0.7978845608028654 * (x + 0.044715 * (x * x * x))))


def _norm_modulate(x, g, shift, scale):
    ms = jnp.mean(x * x, axis=-1, keepdims=True)
    return (x * lax.rsqrt(ms + EPS) * g) * (1.0 + scale) + shift


def _modulation_kernel(cs_ref, w_ref, b_ref, o_ref):
    n_rows = cs_ref.shape[0]
    nb = w_ref.shape[1]
    rows = []
    for r in range(n_rows):
        c = cs_ref[r]
        s = c * jax.nn.sigmoid(c)
        parts = [jnp.sum(w_ref[:, n * V7X_LANES:(n + 1) * V7X_LANES] * s, axis=0, keepdims=True)
                 for n in range(nb // V7X_LANES)]
        rows.append(jnp.concatenate(parts, axis=1) + b_ref[...])
    rows.append(jnp.zeros((V7X_SUBLANES - n_rows, nb), F32))
    o_ref[...] = jnp.concatenate(rows, axis=0)


def _modulation(cs, w_mod, b_mod):
    n_rows, d = cs.shape
    n = w_mod.shape[1]
    nb = 1024
    csb = jnp.broadcast_to(cs[:, :, None], (n_rows, d, V7X_LANES))
    out = pl.pallas_call(
        _modulation_kernel,
        grid=(n // nb,),
        in_specs=[
            pl.BlockSpec((n_rows, d, V7X_LANES), lambda j: (0, 0, 0)),
            pl.BlockSpec((d, nb), lambda j: (0, j)),
            pl.BlockSpec((1, nb), lambda j: (0, j)),
        ],
        out_specs=pl.BlockSpec((V7X_SUBLANES, nb), lambda j: (0, j)),
        out_shape=jax.ShapeDtypeStruct((V7X_SUBLANES, n), F32),
        compiler_params=_params("arbitrary"),
        name="modulation",
    )(csb, w_mod, b_mod.reshape(1, n))
    return out[:n_rows]


def _norm_proj_kernel(x_ref, g_ref, shift_ref, scale_ref, w_ref, o_ref):
    hn = _norm_modulate(x_ref[...], g_ref[...], shift_ref[0], scale_ref[0])
    o_ref[...] = jnp.dot(hn.astype(BF16), w_ref[...], preferred_element_type=F32)


def _norm_proj(x2, g, shift, scale, w, rows_per_vec, tm):
    r, d = x2.shape
    n = w.shape[1]
    tn = 1024
    tiles_per_vec = rows_per_vec // tm
    return pl.pallas_call(
        _norm_proj_kernel,
        grid=(r // tm, n // tn),
        in_specs=[
            pl.BlockSpec((tm, d), lambda i, j: (i, 0)),
            pl.BlockSpec((1, d), lambda i, j: (0, 0)),
            pl.BlockSpec((1, 1, d), lambda i, j: (i // tiles_per_vec, 0, 0)),
            pl.BlockSpec((1, 1, d), lambda i, j: (i // tiles_per_vec, 0, 0)),
            pl.BlockSpec((d, tn), lambda i, j: (0, j)),
        ],
        out_specs=pl.BlockSpec((tm, tn), lambda i, j: (i, j)),
        out_shape=jax.ShapeDtypeStruct((r, n), F32),
        compiler_params=_params("arbitrary", "arbitrary"),
        name="norm_proj",
    )(x2, g, shift, scale, w)


def _lru_scan_kernel(cur_ref, prev_ref, next_ref, cw_ref, cb_ref, wa_ref, ba_ref, wx_ref, bx_ref,
                     lam_ref, h0_ref, y_ref, xe_ref, a_ref, b_ref, h_ref, *, reverse, n_tiles):
    i = pl.program_id(1)
    ti = (n_tiles - 1 - i) if reverse else i
    t_rows, width = a_ref.shape
    heads, head_dim = wa_ref.shape[0], wa_ref.shape[1]
    halo = V7X_SUBLANES

    @pl.when(i == 0)
    def _():
        h_ref[...] = h0_ref[0]

    zero = jnp.zeros((halo, width), F32)
    xe_ref[0:halo, :] = jnp.where(ti == 0, zero, prev_ref[0])
    xe_ref[halo:halo + t_rows, :] = cur_ref[0]
    xe_ref[halo + t_rows:, :] = jnp.where(ti == n_tiles - 1, zero, next_ref[0])

    xc = jnp.broadcast_to(cb_ref[...], (t_rows, width))
    for k in range(cw_ref.shape[0]):
        lo = halo - CONV_LEFT + k
        xc = xc + xe_ref[lo:lo + t_rows, :] * cw_ref[k:k + 1, :]

    xcb = xc.astype(BF16)
    r_parts, i_parts = [], []
    for h in range(heads):
        xh = xcb[:, h * head_dim:(h + 1) * head_dim]
        r_parts.append(jnp.dot(xh, wa_ref[h], preferred_element_type=F32))
        i_parts.append(jnp.dot(xh, wx_ref[h], preferred_element_type=F32))
    r = jax.nn.sigmoid(jnp.concatenate(r_parts, axis=1) + ba_ref[...])
    ig = jax.nn.sigmoid(jnp.concatenate(i_parts, axis=1) + bx_ref[...])
    lam = lam_ref[...]
    softplus_neg_lam = jnp.maximum(-lam, 0.0) + jnp.log1p(jnp.exp(-jnp.abs(lam)))
    log_a = -LRU_C * r * softplus_neg_lam
    a_ref[...] = jnp.exp(log_a)
    th = jnp.tanh(log_a)
    b_ref[...] = jnp.sqrt(-2.0 * th / (1.0 - th)) * (ig * xc)

    def step(s, h):
        t = (t_rows - 1 - s) if reverse else s
        h = a_ref[pl.ds(t, 1), :] * h + b_ref[pl.ds(t, 1), :]
        y_ref[0, pl.ds(t, 1), :] = h
        return h

    h_ref[...] = lax.fori_loop(0, t_rows, step, h_ref[...], unroll=8)


def _lru_scan(px, conv_w, conv_b, wa, ba, wx, bx, lam, h0, reverse, t_rows):
    bsz, seq, _ = px.shape
    width = conv_w.shape[1]
    n_tiles = seq // t_rows
    halo = V7X_SUBLANES
    tb = t_rows // halo
    last_halo_block = seq // halo - 1

    def tile(i):
        return (n_tiles - 1 - i) if reverse else i

    vec = lambda b, i: (0, 0)
    kern = functools.partial(_lru_scan_kernel, reverse=reverse, n_tiles=n_tiles)
    return pl.pallas_call(
        kern,
        grid=(bsz, n_tiles),
        in_specs=[
            pl.BlockSpec((1, t_rows, width), lambda b, i: (b, tile(i), 0)),
            pl.BlockSpec((1, halo, width), lambda b, i: (b, jnp.maximum(tile(i) * tb - 1, 0), 0)),
            pl.BlockSpec((1, halo, width), lambda b, i: (b, jnp.minimum((tile(i) + 1) * tb, last_halo_block), 0)),
            pl.BlockSpec(conv_w.shape, vec),
            pl.BlockSpec((1, width), vec),
            pl.BlockSpec(wa.shape, lambda b, i: (0, 0, 0)),
            pl.BlockSpec((1, width), vec),
            pl.BlockSpec(wx.shape, lambda b, i: (0, 0, 0)),
            pl.BlockSpec((1, width), vec),
            pl.BlockSpec((1, width), vec),
            pl.BlockSpec((1, 1, width), lambda b, i: (b, 0, 0)),
        ],
        out_specs=pl.BlockSpec((1, t_rows, width), lambda b, i: (b, tile(i), 0)),
        out_shape=jax.ShapeDtypeStruct((bsz, seq, width), F32),
        scratch_shapes=[
            pltpu.VMEM((t_rows + 2 * halo, width), F32),
            pltpu.VMEM((t_rows, width), F32),
            pltpu.VMEM((t_rows, width), F32),
            pltpu.VMEM((1, width), F32),
        ],
        compiler_params=_params("arbitrary", "arbitrary"),
        name="lru_scan_bwd" if reverse else "lru_scan_fwd",
    )(px, px, px, conv_w, conv_b.reshape(1, width), wa, ba.reshape(1, width), wx, bx.reshape(1, width),
      lam.reshape(1, width), h0)


def _mixer_kernel(x_ref, gate_ref, u_ref, v_ref, yf_ref, yb_ref, sw_ref, sb_ref, wout_ref, g1_ref, o_ref,
                  ycat_ref):
    t_rows = x_ref.shape[1]
    lru_width = gate_ref.shape[2]
    groups, group_dim = sw_ref.shape[0], u_ref.shape[2] // sw_ref.shape[0]

    ycat_ref[:, 0:lru_width] = (_gelu(gate_ref[0]) * (yf_ref[0] + yb_ref[0])).astype(BF16)
    for n in range(t_rows // CHUNK):
        rows = slice(n * CHUNK, (n + 1) * CHUNK)
        for g in range(groups):
            cols = slice(g * group_dim, (g + 1) * group_dim)
            vg = _gelu(v_ref[0, rows, cols])
            mu = jnp.mean(vg, axis=-1, keepdims=True)
            dv = vg - mu
            var = jnp.mean(dv * dv, axis=-1, keepdims=True)
            vn = dv * lax.rsqrt(var + EPS)
            mixed = jnp.dot(sw_ref[g], vn.astype(BF16), preferred_element_type=F32) + sb_ref[g]
            ycat_ref[rows, lru_width + g * group_dim:lru_width + (g + 1) * group_dim] = (
                _gelu(u_ref[0, rows, cols]) * mixed).astype(BF16)
    y = jnp.dot(ycat_ref[...], wout_ref[...], preferred_element_type=F32)
    o_ref[0] = x_ref[0] + g1_ref[0] * y


def _mixer(x, proj, y_fwd, y_bwd, sgu_w, sgu_b_bcast, w_out, gate1, t_rows):
    bsz, seq, d = x.shape
    width = y_fwd.shape[2]
    tile3 = lambda c: pl.BlockSpec((1, t_rows, width), lambda b, i: (b, i, c))
    return pl.pallas_call(
        _mixer_kernel,
        grid=(bsz, seq // t_rows),
        in_specs=[
            pl.BlockSpec((1, t_rows, d), lambda b, i: (b, i, 0)),
            tile3(1), tile3(2), tile3(3),
            tile3(0), tile3(0),
            pl.BlockSpec(sgu_w.shape, lambda b, i: (0, 0, 0)),
            pl.BlockSpec(sgu_b_bcast.shape, lambda b, i: (0, 0, 0)),
            pl.BlockSpec(w_out.shape, lambda b, i: (0, 0)),
            pl.BlockSpec((1, 1, d), lambda b, i: (b, 0, 0)),
        ],
        out_specs=pl.BlockSpec((1, t_rows, d), lambda b, i: (b, i, 0)),
        out_shape=jax.ShapeDtypeStruct((bsz, seq, d), F32),
        scratch_shapes=[pltpu.VMEM((t_rows, w_out.shape[0]), BF16)],
        compiler_params=_params("arbitrary", "arbitrary"),
        name="mixer",
    )(x, proj, proj, proj, y_fwd, y_bwd, sgu_w, sgu_b_bcast, w_out, gate1)


def _oddeven_mergesort_pairs(n):
    pairs = []
    p = 1
    while p < n:
        k = p
        while k >= 1:
            for j in range(k % p, n - k, 2 * k):
                for i in range(min(k, n - j - k)):
                    if (i + j) // (2 * p) == (i + j + k) // (2 * p):
                        pairs.append((i + j, i + j + k))
            k //= 2
        p *= 2
    return pairs


def _bitonic_merge_pairs(n):
    pairs = []
    d = n // 2
    while d >= 1:
        pairs.extend((i, i + d) for i in range(n) if not i & d)
        d //= 2
    return pairs


def _compare_exchange(v, pairs):
    for i, j in pairs:
        a, b = v[i], v[j]
        if b is None:
            continue
        if a is None:
            v[i], v[j] = b, None
        else:
            v[i], v[j] = jnp.maximum(a, b), jnp.minimum(a, b)


def _top16_groups(groups):
    v = list(groups)
    n = len(v)
    _compare_exchange(v, _oddeven_mergesort_pairs(n))
    shift = V7X_SUBLANES // 2
    while shift >= 1:
        merged = []
        for i in range(n):
            a, b = v[i], v[n - 1 - i]
            b = None if b is None else pltpu.roll(b, shift, 0)
            merged.append(b if a is None else (a if b is None else jnp.maximum(a, b)))
        v = merged
        _compare_exchange(v, _bitonic_merge_pairs(n))
        shift //= 2
    return v


def _sublane_gather(groups, sub):
    out = groups[-1]
    for j in range(len(groups) - 2, -1, -1):
        out = jnp.where(sub == j, groups[j], out)
    return out


def _pair_candidates(v1, v2, sub):
    ns = V7X_SUBLANES
    neg_inf = jnp.float32(-jnp.inf)
    v2_lo = _sublane_gather(v2[:ns], sub)
    v2_hi = _sublane_gather(v2[ns:], sub)
    v1_hi = _sublane_gather(v1[ns:], sub)
    groups = [v1[0] + v2_lo, v1[0] + v2_hi, v1[1] + v2_lo]
    for i in range(2, ns):
        groups.append(jnp.where(sub < PEER_TOPK // (i + 1), v1[i] + v2_lo, neg_inf))
    groups.append(v1_hi + v2[0])
    return groups


def _monotone_count(v, pred):
    w = jnp.where
    c1 = pred(v[7])
    c2 = pred(w(c1, v[11], v[3]))
    c3 = pred(w(c1, w(c2, v[13], v[9]), w(c2, v[5], v[1])))
    c4 = pred(w(c1, w(c2, w(c3, v[14], v[12]), w(c3, v[10], v[8])),
                w(c2, w(c3, v[6], v[4]), w(c3, v[2], v[0]))))
    c5 = pred(v[15])
    return (w(c1, 8.0, 0.0) + w(c2, 4.0, 0.0)) + (w(c3, 2.0, 0.0) + w(c4, 1.0, 0.0)) + w(c5, 1.0, 0.0)


def _bf16_pair_words(x):
    bits = pltpu.bitcast(x.astype(BF16).astype(F32), U32)
    return bits | lax.shift_right_logical(bits, jnp.uint32(16))


def _peer_route_kernel(x_ref, g_ref, shift_ref, scale_ref, wqt_ref, k1_ref, k2_ref,
                       ht_ref, rank2_ref, p2_ref, cntw_ref, p1w_ref):
    heads, n_keys, half = k1_ref.shape
    n_tok = x_ref.shape[0]
    ns = V7X_SUBLANES
    n_groups = n_keys // ns
    hn = _norm_modulate(x_ref[...], g_ref[...], shift_ref[0], scale_ref[0])
    ht = hn.T.astype(BF16)
    ht_ref[...] = ht
    qt = jnp.dot(wqt_ref[...], ht, preferred_element_type=F32)
    lanes = V7X_LANES
    sub = lax.broadcasted_iota(jnp.int32, (ns, lanes), 0)
    for h in range(heads):
        q1 = qt[(2 * h) * half:(2 * h + 1) * half].astype(BF16)
        q2 = qt[(2 * h + 1) * half:(2 * h + 2) * half].astype(BF16)
        s1_all = jnp.dot(k1_ref[h], q1, preferred_element_type=F32)
        s2_all = jnp.dot(k2_ref[h], q2, preferred_element_type=F32)
        for c in range(n_tok // lanes):
            cols = slice(c * lanes, (c + 1) * lanes)
            s1, s2 = s1_all[:, cols], s2_all[:, cols]
            s1g = s1.reshape(n_groups, ns, lanes)
            s2g = s2.reshape(n_groups, ns, lanes)
            v1 = _top16_groups([s1g[i] for i in range(n_groups)])
            v2 = _top16_groups([s2g[i] for i in range(n_groups)])
            cand = _pair_candidates(v1, v2, sub)
            tau = _top16_groups(cand + [None] * (n_groups - len(cand)))[PEER_TOPK - 1]
            m = v1[0] + v2[0]
            zs = None
            for cg in cand:
                term = jnp.where(cg >= tau, jnp.exp(cg - m), 0.0)
                zs = term if zs is None else zs + term
            inv_z = 1.0 / jnp.sum(zs, axis=0, keepdims=True)
            v2b = [v[None] for v in v2]
            rank2 = _monotone_count(v2b, lambda t: t > s2g)
            cnt = _monotone_count(v2b, lambda t: s1g + t >= tau[None])
            rank2_ref[h, :, cols] = rank2.reshape(n_keys, lanes).astype(BF16)
            p2_ref[h, :, cols] = jnp.exp(s2 - v2[0][0:1]).astype(BF16)
            cntw_ref[h, :, cols] = _bf16_pair_words(cnt.reshape(n_keys, lanes))
            p1w_ref[h, :, cols] = _bf16_pair_words(jnp.exp(s1 - v1[0][0:1]) * inv_z)


def _peer_route(x2, g, shift, scale, wq_t, k1, k2, rows_per_vec, tm):
    n, d = x2.shape
    heads, n_keys, _ = k1.shape
    tiles_per_vec = rows_per_vec // tm
    tok_block = pl.BlockSpec((heads, n_keys, tm), lambda i: (0, 0, i))
    return pl.pallas_call(
        _peer_route_kernel,
        grid=(n // tm,),
        in_specs=[
            pl.BlockSpec((tm, d), lambda i: (i, 0)),
            pl.BlockSpec((1, d), lambda i: (0, 0)),
            pl.BlockSpec((1, 1, d), lambda i: (i // tiles_per_vec, 0, 0)),
            pl.BlockSpec((1, 1, d), lambda i: (i // tiles_per_vec, 0, 0)),
            pl.BlockSpec(wq_t.shape, lambda i: (0, 0)),
            pl.BlockSpec(k1.shape, lambda i: (0, 0, 0)),
            pl.BlockSpec(k2.shape, lambda i: (0, 0, 0)),
        ],
        out_specs=[pl.BlockSpec((d, tm), lambda i: (0, i)), tok_block, tok_block, tok_block, tok_block],
        out_shape=[
            jax.ShapeDtypeStruct((d, n), BF16),
            jax.ShapeDtypeStruct((heads, n_keys, n), BF16),
            jax.ShapeDtypeStruct((heads, n_keys, n), BF16),
            jax.ShapeDtypeStruct((heads, n_keys, n), U32),
            jax.ShapeDtypeStruct((heads, n_keys, n), U32),
        ],
        compiler_params=_params("arbitrary"),
        name="peer_route",
    )(x2, g, shift, scale, wq_t, k1, k2)


def _peer_experts_kernel(ht_ref, rank2_ref, p2_ref, cntw_ref, p1w_ref, u_ref, vt_ref, o_ref, wa_cur_ref, wa_next_ref,
                         *, n_blocks, n_exp_blocks):
    g = pl.program_id(0)
    heads, n_keys, n_tok = rank2_ref.shape
    e1_per_block = u_ref.shape[0] // n_keys
    nb = V7X_BF16_ROWS
    ns = V7X_SUBLANES
    eb_act = lax.rem(jnp.minimum(g, n_blocks - 1), n_exp_blocks)
    eb_val = lax.rem(jnp.maximum(g - 1, 0), n_exp_blocks)

    @pl.when(g == 0)
    def _():
        wa_next_ref[...] = jnp.zeros_like(wa_next_ref)

    @pl.when(eb_val == 0)
    def _():
        o_ref[...] = jnp.zeros_like(o_ref)

    wa_cur_ref[...] = wa_next_ref[...]
    row0 = pl.multiple_of(jnp.minimum(g, 0), V7X_BF16_ROWS)

    n_exp_sub = 2
    e1_per_sub = e1_per_block // n_exp_sub
    eb_sub = e1_per_sub * n_keys

    def weighted_activation(a_t, xs):
        blocks = []
        for j in range(e1_per_sub):
            e1 = eb_act * e1_per_block + xs * e1_per_sub + j
            cnt = [pltpu.bitcast(jnp.broadcast_to(cntw_ref[h, pl.ds(e1, 1), :], (ns, n_tok)), BF16)
                   for h in range(heads)]
            p1 = [pltpu.bitcast(jnp.broadcast_to(p1w_ref[h, pl.ds(e1, 1), :], (ns, n_tok)), BF16)
                  for h in range(heads)]
            for rg in range(n_keys // nb):
                rows = slice(rg * nb, (rg + 1) * nb)
                w = None
                for h in range(heads):
                    wh = jnp.where(rank2_ref[h, rows, :] < cnt[h], p2_ref[h, rows, :] * p1[h],
                                   jnp.zeros((nb, n_tok), BF16))
                    w = wh if w is None else w + wh
                act = _gelu(a_t[j * n_keys + rg * nb:j * n_keys + (rg + 1) * nb])
                blocks.append(w * act.astype(BF16))
        return jnp.concatenate(blocks, axis=0)

    pre = [jnp.dot(u_ref[xs * eb_sub:(xs + 1) * eb_sub, :], ht_ref[...], preferred_element_type=F32)
           for xs in range(n_exp_sub)]
    o_ref[...] += jnp.dot(vt_ref[...], wa_cur_ref[pl.ds(row0, wa_cur_ref.shape[0]), :],
                          preferred_element_type=F32)
    for xs in range(n_exp_sub):
        wa_next_ref[xs * eb_sub:(xs + 1) * eb_sub, :] = weighted_activation(pre[xs], xs)


def _peer_experts(ht, rank2, p2, cntw, p1w, u_bf, vt_bf, tm, eb):
    d, n = ht.shape
    heads, n_keys, _ = rank2.shape
    n_exp_blocks = u_bf.shape[0] // eb
    n_blocks = (n // tm) * n_exp_blocks

    def act_block(g):
        return jnp.minimum(g, n_blocks - 1)

    def val_block(g):
        return jnp.maximum(g - 1, 0)

    tok_block = pl.BlockSpec((heads, n_keys, tm), lambda g: (0, 0, act_block(g) // n_exp_blocks))
    kern = functools.partial(_peer_experts_kernel, n_blocks=n_blocks, n_exp_blocks=n_exp_blocks)
    return pl.pallas_call(
        kern,
        grid=(n_blocks + 1,),
        in_specs=[
            pl.BlockSpec((d, tm), lambda g: (0, act_block(g) // n_exp_blocks)),
            tok_block, tok_block, tok_block, tok_block,
            pl.BlockSpec((eb, d), lambda g: (act_block(g) % n_exp_blocks, 0)),
            pl.BlockSpec((d, eb), lambda g: (0, val_block(g) % n_exp_blocks)),
        ],
        out_specs=pl.BlockSpec((d, tm), lambda g: (0, val_block(g) // n_exp_blocks)),
        out_shape=jax.ShapeDtypeStruct((d, n), F32),
        scratch_shapes=[pltpu.VMEM((eb, tm), BF16), pltpu.VMEM((eb, tm), BF16)],
        compiler_params=_params("arbitrary"),
        name="peer_experts",
    )(ht, rank2, p2, cntw, p1w, u_bf, vt_bf)


def _final_kernel(x_ref, pt_ref, gate_ref, g_ref, o_ref):
    x = x_ref[...] + gate_ref[0] * pt_ref[...].T
    ms = jnp.mean(x * x, axis=-1, keepdims=True)
    o_ref[...] = x * lax.rsqrt(ms + EPS) * g_ref[...]


def _final(x2, peer_t, gate, g, rows_per_vec, tm):
    n, d = x2.shape
    tiles_per_vec = rows_per_vec // tm
    return pl.pallas_call(
        _final_kernel,
        grid=(n // tm,),
        in_specs=[
            pl.BlockSpec((tm, d), lambda i: (i, 0)),
            pl.BlockSpec((d, tm), lambda i: (0, i)),
            pl.BlockSpec((1, 1, d), lambda i: (i // tiles_per_vec, 0, 0)),
            pl.BlockSpec((1, d), lambda i: (0, 0)),
        ],
        out_specs=pl.BlockSpec((tm, d), lambda i: (i, 0)),
        out_shape=jax.ShapeDtypeStruct((n, d), F32),
        compiler_params=_params("arbitrary"),
        name="final_norm",
    )(x2, peer_t, gate, g)


def kernel(x, c, ctx, c_ctx, w_mod, b_mod, norm1_g, norm2_g, w_in, conv_w, conv_b, lru_wa, lru_ba, lru_wx,
           lru_bx, lru_lambda, sgu_w, sgu_b, w_out, peer_wq, peer_k1, peer_k2, peer_u, peer_v, final_g):
    bsz, seq, d = x.shape
    ctx_len = ctx.shape[1]
    depth = w_mod.shape[0]
    assert depth == 1, "only the single-layer configuration of the reference is implemented"
    lru_width = conv_w.shape[2]
    assert seq % 512 == 0 and ctx_len % CHUNK == 0 and d % V7X_LANES == 0

    t_seq = 512
    t_tok = 512
    expert_block = 1024

    w_in_bf = w_in[0].astype(BF16)
    w_out_bf = w_out[0].astype(BF16)
    wa_bf, wx_bf = lru_wa[0].astype(BF16), lru_wx[0].astype(BF16)
    sgu_w_bf = sgu_w[0].astype(BF16)
    sgu_b_bcast = jnp.broadcast_to(sgu_b[0][:, :, None], sgu_b.shape[1:] + (sgu_w.shape[-1],))
    wq_t_bf = peer_wq[0].T.astype(BF16)
    k1_bf, k2_bf = peer_k1[0].astype(BF16), peer_k2[0].astype(BF16)
    u_bf = peer_u[0].astype(BF16)
    vt_bf = peer_v[0].T.astype(BF16)

    mod = _modulation(jnp.concatenate([c, c_ctx[None, :]], axis=0), w_mod[0], b_mod[0])
    mod = mod.reshape(bsz + 1, N_MOD, 1, d)
    lat = lambda k: mod[:bsz, k]
    ctx_vec = lambda k: mod[bsz:, k]
    g1 = norm1_g[0].reshape(1, d)
    g2 = norm2_g[0].reshape(1, d)

    px_ctx = _norm_proj(ctx.reshape(bsz * ctx_len, d), g1, ctx_vec(0), ctx_vec(1), w_in_bf[:, :lru_width],
                        rows_per_vec=bsz * ctx_len, tm=ctx_len).reshape(bsz, ctx_len, lru_width)
    proj = _norm_proj(x.reshape(bsz * seq, d), g1, lat(0), lat(1), w_in_bf,
                      rows_per_vec=seq, tm=1024).reshape(bsz, seq, w_in.shape[2])

    zeros_h = jnp.zeros((bsz, 1, lru_width), F32)
    ys = []
    for direction, reverse in enumerate((False, True)):
        prm = (conv_w[0], conv_b[0], wa_bf[direction], lru_ba[0, direction], wx_bf[direction],
               lru_bx[0, direction], lru_lambda[0, direction])
        y_ctx = _lru_scan(px_ctx, *prm, zeros_h, reverse, ctx_len)
        h_ctx = y_ctx[:, 0:1] if reverse else y_ctx[:, ctx_len - 1:ctx_len]
        ys.append(_lru_scan(proj, *prm, h_ctx, reverse, t_seq))

    x_mid = _mixer(x, proj, ys[0], ys[1], sgu_w_bf, sgu_b_bcast, w_out_bf, lat(2), t_seq)

    x2 = x_mid.reshape(bsz * seq, d)
    ht, rank2, p2, cntw, p1w = _peer_route(x2, g2, lat(3), lat(4), wq_t_bf, k1_bf, k2_bf,
                                           rows_per_vec=seq, tm=t_tok)
    peer_t = _peer_experts(ht, rank2, p2, cntw, p1w, u_bf, vt_bf, t_tok, expert_block)
    out = _final(x2, peer_t, lat(5), final_g.reshape(1, d), rows_per_vec=seq, tm=512)
    return out.reshape(bsz, seq, d)
```

```python
import functools

import jax
import jax.numpy as jnp
from jax import lax
from jax.experimental import pallas as pl
from jax.experimental.pallas import tpu as pltpu

F32 = jnp.float32
BF16 = jnp.bfloat16
U32 = jnp.uint32

EPS = 1e-6
LRU_C = 8.0
CONV_LEFT = 2
CHUNK = 128
PEER_TOPK = 16
N_MOD = 6

V7X_LANES = 128
V7X_SUBLANES = 8
V7X_BF16_ROWS = 16
V7X_MXU_COLS = 256
V7X_VMEM_LIMIT_BYTES = 56 * 1024 * 1024


def _params(*semantics):
    return pltpu.CompilerParams(dimension_semantics=semantics, vmem_limit_bytes=V7X_VMEM_LIMIT_BYTES)


def _gelu(x):
    return 0.5 * x * (1.0 + jnp.tanh(0.7978845608028654 * (x + 0.044715 * (x * x * x))))


def _norm_modulate(x, g, shift, scale):
    ms = jnp.mean(x * x, axis=-1, keepdims=True)
    return (x * lax.rsqrt(ms + EPS) * g) * (1.0 + scale) + shift


def _modulation_kernel(cs_ref, w_ref, b_ref, o_ref):
    n_rows = cs_ref.shape[0]
    nb = w_ref.shape[1]
    rows = []
    for r in range(n_rows):
        c = cs_ref[r]
        s = c * jax.nn.sigmoid(c)
        parts = [jnp.sum(w_ref[:, n * V7X_LANES:(n + 1) * V7X_LANES] * s, axis=0, keepdims=True)
                 for n in range(nb // V7X_LANES)]
        rows.append(jnp.concatenate(parts, axis=1) + b_ref[...])
    rows.append(jnp.zeros((V7X_SUBLANES - n_rows, nb), F32))
    o_ref[...] = jnp.concatenate(rows, axis=0)


def _modulation(cs, w_mod, b_mod):
    n_rows, d = cs.shape
    n = w_mod.shape[1]
    nb = 1024
    csb = jnp.broadcast_to(cs[:, :, None], (n_rows, d, V7X_LANES))
    out = pl.pallas_call(
        _modulation_kernel,
        grid=(n // nb,),
        in_specs=[
            pl.BlockSpec((n_rows, d, V7X_LANES), lambda j: (0, 0, 0)),
            pl.BlockSpec((d, nb), lambda j: (0, j)),
            pl.BlockSpec((1, nb), lambda j: (0, j)),
        ],
        out_specs=pl.BlockSpec((V7X_SUBLANES, nb), lambda j: (0, j)),
        out_shape=jax.ShapeDtypeStruct((V7X_SUBLANES, n), F32),
        compiler_params=_params("arbitrary"),
        name="modulation",
    )(csb, w_mod, b_mod.reshape(1, n))
    return out[:n_rows]


def _norm_proj_kernel(x_ref, g_ref, shift_ref, scale_ref, w_ref, o_ref, hn_ref):
    @pl.when(pl.program_id(1) == 0)
    def _():
        hn_ref[...] = _norm_modulate(x_ref[...], g_ref[...], shift_ref[0], scale_ref[0]).astype(BF16)

    o_ref[...] = jnp.dot(hn_ref[...], w_ref[...], preferred_element_type=F32)


def _norm_proj(x2, g, shift, scale, w, rows_per_vec, tm):
    r, d = x2.shape
    n = w.shape[1]
    tn = 1024
    tiles_per_vec = rows_per_vec // tm
    return pl.pallas_call(
        _norm_proj_kernel,
        grid=(r // tm, n // tn),
        in_specs=[
            pl.BlockSpec((tm, d), lambda i, j: (i, 0)),
            pl.BlockSpec((1, d), lambda i, j: (0, 0)),
            pl.BlockSpec((1, 1, d), lambda i, j: (i // tiles_per_vec, 0, 0)),
            pl.BlockSpec((1, 1, d), lambda i, j: (i // tiles_per_vec, 0, 0)),
            pl.BlockSpec((d, tn), lambda i, j: (0, j)),
        ],
        out_specs=pl.BlockSpec((tm, tn), lambda i, j: (i, j)),
        out_shape=jax.ShapeDtypeStruct((r, n), F32),
        scratch_shapes=[pltpu.VMEM((tm, d), BF16)],
        compiler_params=_params("arbitrary", "arbitrary"),
        name="norm_proj",
    )(x2, g, shift, scale, w)


def _lru_scan_kernel(cur_ref, prev_ref, next_ref, cw_ref, cb_ref, wa_ref, ba_ref, wx_ref, bx_ref,
                     lam_ref, h0_ref, y_ref, xe_ref, a_ref, b_ref, h_ref, *, reverse, n_tiles):
    i = pl.program_id(1)
    ti = (n_tiles - 1 - i) if reverse else i
    t_rows, width = a_ref.shape
    heads, head_dim = wa_ref.shape[0], wa_ref.shape[1]
    halo = V7X_SUBLANES

    @pl.when(i == 0)
    def _():
        h_ref[...] = h0_ref[0]

    zero = jnp.zeros((halo, width), F32)
    xe_ref[0:halo, :] = jnp.where(ti == 0, zero, prev_ref[0])
    xe_ref[halo:halo + t_rows, :] = cur_ref[0]
    xe_ref[halo + t_rows:, :] = jnp.where(ti == n_tiles - 1, zero, next_ref[0])

    xc = jnp.broadcast_to(cb_ref[...], (t_rows, width))
    for k in range(cw_ref.shape[0]):
        lo = halo - CONV_LEFT + k
        xc = xc + xe_ref[lo:lo + t_rows, :] * cw_ref[k:k + 1, :]

    xcb = xc.astype(BF16)
    r_parts, i_parts = [], []
    for h in range(heads):
        xh = xcb[:, h * head_dim:(h + 1) * head_dim]
        r_parts.append(jnp.dot(xh, wa_ref[h], preferred_element_type=F32))
        i_parts.append(jnp.dot(xh, wx_ref[h], preferred_element_type=F32))
    r = jax.nn.sigmoid(jnp.concatenate(r_parts, axis=1) + ba_ref[...])
    ig = jax.nn.sigmoid(jnp.concatenate(i_parts, axis=1) + bx_ref[...])
    lam = lam_ref[...]
    softplus_neg_lam = jnp.maximum(-lam, 0.0) + jnp.log1p(jnp.exp(-jnp.abs(lam)))
    log_a = -LRU_C * r * softplus_neg_lam
    a_ref[...] = jnp.exp(log_a)
    th = jnp.tanh(log_a)
    b_ref[...] = jnp.sqrt(-2.0 * th / (1.0 - th)) * (ig * xc)

    def step(s, h):
        t = (t_rows - 1 - s) if reverse else s
        h = a_ref[pl.ds(t, 1), :] * h + b_ref[pl.ds(t, 1), :]
        y_ref[0, pl.ds(t, 1), :] = h
        return h

    h_ref[...] = lax.fori_loop(0, t_rows, step, h_ref[...], unroll=8)


def _lru_scan(px, conv_w, conv_b, wa, ba, wx, bx, lam, h0, reverse, t_rows):
    bsz, seq, _ = px.shape
    width = conv_w.shape[1]
    n_tiles = seq // t_rows
    halo = V7X_SUBLANES
    tb = t_rows // halo
    last_halo_block = seq // halo - 1

    def tile(i):
        return (n_tiles - 1 - i) if reverse else i

    vec = lambda b, i: (0, 0)
    kern = functools.partial(_lru_scan_kernel, reverse=reverse, n_tiles=n_tiles)
    return pl.pallas_call(
        kern,
        grid=(bsz, n_tiles),
        in_specs=[
            pl.BlockSpec((1, t_rows, width), lambda b, i: (b, tile(i), 0)),
            pl.BlockSpec((1, halo, width), lambda b, i: (b, jnp.maximum(tile(i) * tb - 1, 0), 0)),
            pl.BlockSpec((1, halo, width), lambda b, i: (b, jnp.minimum((tile(i) + 1) * tb, last_halo_block), 0)),
            pl.BlockSpec(conv_w.shape, vec),
            pl.BlockSpec((1, width), vec),
            pl.BlockSpec(wa.shape, lambda b, i: (0, 0, 0)),
            pl.BlockSpec((1, width), vec),
            pl.BlockSpec(wx.shape, lambda b, i: (0, 0, 0)),
            pl.BlockSpec((1, width), vec),
            pl.BlockSpec((1, width), vec),
            pl.BlockSpec((1, 1, width), lambda b, i: (b, 0, 0)),
        ],
        out_specs=pl.BlockSpec((1, t_rows, width), lambda b, i: (b, tile(i), 0)),
        out_shape=jax.ShapeDtypeStruct((bsz, seq, width), F32),
        scratch_shapes=[
            pltpu.VMEM((t_rows + 2 * halo, width), F32),
            pltpu.VMEM((t_rows, width), F32),
            pltpu.VMEM((t_rows, width), F32),
            pltpu.VMEM((1, width), F32),
        ],
        compiler_params=_params("arbitrary", "arbitrary"),
        name="lru_scan_bwd" if reverse else "lru_scan_fwd",
    )(px, px, px, conv_w, conv_b.reshape(1, width), wa, ba.reshape(1, width), wx, bx.reshape(1, width),
      lam.reshape(1, width), h0)


def _mixer_kernel(x_ref, gate_ref, u_ref, v_ref, yf_ref, yb_ref, sw_ref, sb_ref, wout_ref, g1_ref, o_ref,
                  ycat_ref):
    t_rows = x_ref.shape[1]
    lru_width = gate_ref.shape[2]
    groups, group_dim = sw_ref.shape[0], u_ref.shape[2] // sw_ref.shape[0]

    ycat_ref[:, 0:lru_width] = (_gelu(gate_ref[0]) * (yf_ref[0] + yb_ref[0])).astype(BF16)
    for n in range(t_rows // CHUNK):
        rows = slice(n * CHUNK, (n + 1) * CHUNK)
        for g in range(groups):
            cols = slice(g * group_dim, (g + 1) * group_dim)
            vg = _gelu(v_ref[0, rows, cols])
            mu = jnp.mean(vg, axis=-1, keepdims=True)
            dv = vg - mu
            var = jnp.mean(dv * dv, axis=-1, keepdims=True)
            vn = dv * lax.rsqrt(var + EPS)
            mixed = jnp.dot(sw_ref[g], vn.astype(BF16), preferred_element_type=F32) + sb_ref[g]
            ycat_ref[rows, lru_width + g * group_dim:lru_width + (g + 1) * group_dim] = (
                _gelu(u_ref[0, rows, cols]) * mixed).astype(BF16)
    y = jnp.dot(ycat_ref[...], wout_ref[...], preferred_element_type=F32)
    o_ref[0] = x_ref[0] + g1_ref[0] * y


def _mixer(x, proj, y_fwd, y_bwd, sgu_w, sgu_b_bcast, w_out, gate1, t_rows):
    bsz, seq, d = x.shape
    width = y_fwd.shape[2]
    tile3 = lambda c: pl.BlockSpec((1, t_rows, width), lambda b, i: (b, i, c))
    return pl.pallas_call(
        _mixer_kernel,
        grid=(bsz, seq // t_rows),
        in_specs=[
            pl.BlockSpec((1, t_rows, d), lambda b, i: (b, i, 0)),
            tile3(1), tile3(2), tile3(3),
            tile3(0), tile3(0),
            pl.BlockSpec(sgu_w.shape, lambda b, i: (0, 0, 0)),
            pl.BlockSpec(sgu_b_bcast.shape, lambda b, i: (0, 0, 0)),
            pl.BlockSpec(w_out.shape, lambda b, i: (0, 0)),
            pl.BlockSpec((1, 1, d), lambda b, i: (b, 0, 0)),
        ],
        out_specs=pl.BlockSpec((1, t_rows, d), lambda b, i: (b, i, 0)),
        out_shape=jax.ShapeDtypeStruct((bsz, seq, d), F32),
        scratch_shapes=[pltpu.VMEM((t_rows, w_out.shape[0]), BF16)],
        compiler_params=_params("arbitrary", "arbitrary"),
        name="mixer",
    )(x, proj, proj, proj, y_fwd, y_bwd, sgu_w, sgu_b_bcast, w_out, gate1)


def _oddeven_mergesort_pairs(n):
    pairs = []
    p = 1
    while p < n:
        k = p
        while k >= 1:
            for j in range(k % p, n - k, 2 * k):
                for i in range(min(k, n - j - k)):
                    if (i + j) // (2 * p) == (i + j + k) // (2 * p):
                        pairs.append((i + j, i + j + k))
            k //= 2
        p *= 2
    return pairs


def _bitonic_merge_pairs(n):
    pairs = []
    d = n // 2
    while d >= 1:
        pairs.extend((i, i + d) for i in range(n) if not i & d)
        d //= 2
    return pairs


def _compare_exchange(v, pairs):
    for i, j in pairs:
        a, b = v[i], v[j]
        if b is None:
            continue
        if a is None:
            v[i], v[j] = b, None
        else:
            v[i], v[j] = jnp.maximum(a, b), jnp.minimum(a, b)


def _top16_groups(groups):
    v = list(groups)
    n = len(v)
    _compare_exchange(v, _oddeven_mergesort_pairs(n))
    shift = V7X_SUBLANES // 2
    while shift >= 1:
        merged = []
        for i in range(n):
            a, b = v[i], v[n - 1 - i]
            b = None if b is None else pltpu.roll(b, shift, 0)
            merged.append(b if a is None else (a if b is None else jnp.maximum(a, b)))
        v = merged
        _compare_exchange(v, _bitonic_merge_pairs(n))
        shift //= 2
    return v


def _sublane_gather(groups, sub):
    out = groups[-1]
    for j in range(len(groups) - 2, -1, -1):
        out = jnp.where(sub == j, groups[j], out)
    return out


def _pair_candidates(v1, v2, sub):
    ns = V7X_SUBLANES
    neg_inf = jnp.float32(-jnp.inf)
    v2_lo = _sublane_gather(v2[:ns], sub)
    v2_hi = _sublane_gather(v2[ns:], sub)
    v1_hi = _sublane_gather(v1[ns:], sub)
    groups = [v1[0] + v2_lo, v1[0] + v2_hi, v1[1] + v2_lo]
    for i in range(2, ns):
        groups.append(jnp.where(sub < PEER_TOPK // (i + 1), v1[i] + v2_lo, neg_inf))
    groups.append(v1_hi + v2[0])
    return groups


def _monotone_count(v, pred):
    w = jnp.where
    c1 = pred(v[7])
    c2 = pred(w(c1, v[11], v[3]))
    c3 = pred(w(c1, w(c2, v[13], v[9]), w(c2, v[5], v[1])))
    c4 = pred(w(c1, w(c2, w(c3, v[14], v[12]), w(c3, v[10], v[8])),
                w(c2, w(c3, v[6], v[4]), w(c3, v[2], v[0]))))
    c5 = pred(v[15])
    return (w(c1, 8.0, 0.0) + w(c2, 4.0, 0.0)) + (w(c3, 2.0, 0.0) + w(c4, 1.0, 0.0)) + w(c5, 1.0, 0.0)


def _bf16_pair_words(x):
    bits = pltpu.bitcast(x.astype(BF16).astype(F32), U32)
    return bits | lax.shift_right_logical(bits, jnp.uint32(16))


def _peer_route_kernel(x_ref, g_ref, shift_ref, scale_ref, wqt_ref, k1_ref, k2_ref,
                       ht_ref, rank2_ref, p2_ref, cntw_ref, p1w_ref):
    heads, n_keys, half = k1_ref.shape
    n_tok = x_ref.shape[0]
    ns = V7X_SUBLANES
    n_groups = n_keys // ns
    hn = _norm_modulate(x_ref[...], g_ref[...], shift_ref[0], scale_ref[0])
    ht = hn.T.astype(BF16)
    ht_ref[...] = ht
    qt = jnp.dot(wqt_ref[...], ht, preferred_element_type=F32)
    lanes = V7X_LANES
    sub = lax.broadcasted_iota(jnp.int32, (ns, lanes), 0)
    for h in range(heads):
        q1 = qt[(2 * h) * half:(2 * h + 1) * half].astype(BF16)
        q2 = qt[(2 * h + 1) * half:(2 * h + 2) * half].astype(BF16)
        s1_all = jnp.dot(k1_ref[h], q1, preferred_element_type=F32)
        s2_all = jnp.dot(k2_ref[h], q2, preferred_element_type=F32)
        for c in range(n_tok // lanes):
            cols = slice(c * lanes, (c + 1) * lanes)
            s1, s2 = s1_all[:, cols], s2_all[:, cols]
            s1g = s1.reshape(n_groups, ns, lanes)
            s2g = s2.reshape(n_groups, ns, lanes)
            v1 = _top16_groups([s1g[i] for i in range(n_groups)])
            v2 = _top16_groups([s2g[i] for i in range(n_groups)])
            cand = _pair_candidates(v1, v2, sub)
            tau = _top16_groups(cand + [None] * (n_groups - len(cand)))[PEER_TOPK - 1]
            m = v1[0] + v2[0]
            zs = None
            for cg in cand:
                term = jnp.where(cg >= tau, jnp.exp(cg - m), 0.0)
                zs = term if zs is None else zs + term
            inv_z = 1.0 / jnp.sum(zs, axis=0, keepdims=True)
            v2b = [v[None] for v in v2]
            rank2 = _monotone_count(v2b, lambda t: t > s2g)
            cnt = _monotone_count(v2b, lambda t: s1g + t >= tau[None])
            rank2_ref[h, :, cols] = rank2.reshape(n_keys, lanes).astype(BF16)
            p2_ref[h, :, cols] = jnp.exp(s2 - v2[0][0:1]).astype(BF16)
            cntw_ref[h, :, cols] = _bf16_pair_words(cnt.reshape(n_keys, lanes))
            p1w_ref[h, :, cols] = _bf16_pair_words(jnp.exp(s1 - v1[0][0:1]) * inv_z)


def _peer_route(x2, g, shift, scale, wq_t, k1, k2, rows_per_vec, tm):
    n, d = x2.shape
    heads, n_keys, _ = k1.shape
    tiles_per_vec = rows_per_vec // tm
    tok_block = pl.BlockSpec((heads, n_keys, tm), lambda i: (0, 0, i))
    return pl.pallas_call(
        _peer_route_kernel,
        grid=(n // tm,),
        in_specs=[
            pl.BlockSpec((tm, d), lambda i: (i, 0)),
            pl.BlockSpec((1, d), lambda i: (0, 0)),
            pl.BlockSpec((1, 1, d), lambda i: (i // tiles_per_vec, 0, 0)),
            pl.BlockSpec((1, 1, d), lambda i: (i // tiles_per_vec, 0, 0)),
            pl.BlockSpec(wq_t.shape, lambda i: (0, 0)),
            pl.BlockSpec(k1.shape, lambda i: (0, 0, 0)),
            pl.BlockSpec(k2.shape, lambda i: (0, 0, 0)),
        ],
        out_specs=[pl.BlockSpec((d, tm), lambda i: (0, i)), tok_block, tok_block, tok_block, tok_block],
        out_shape=[
            jax.ShapeDtypeStruct((d, n), BF16),
            jax.ShapeDtypeStruct((heads, n_keys, n), BF16),
            jax.ShapeDtypeStruct((heads, n_keys, n), BF16),
            jax.ShapeDtypeStruct((heads, n_keys, n), U32),
            jax.ShapeDtypeStruct((heads, n_keys, n), U32),
        ],
        compiler_params=_params("arbitrary"),
        name="peer_route",
    )(x2, g, shift, scale, wq_t, k1, k2)


def _peer_experts_kernel(ht_ref, rank2_ref, p2_ref, cntw_ref, p1w_ref, u_ref, vt_ref, o_ref):
    e = pl.program_id(1)
    heads, n_keys, n_tok = rank2_ref.shape
    e1_per_block = u_ref.shape[0] // n_keys
    ts = V7X_MXU_COLS
    nb = V7X_BF16_ROWS
    ns = V7X_SUBLANES

    @pl.when(e == 0)
    def _():
        o_ref[...] = jnp.zeros_like(o_ref)

    n_exp_sub = 2
    e1_per_sub = e1_per_block // n_exp_sub
    eb_sub = e1_per_sub * n_keys

    def pre_activation(blk):
        cols, xs = blk
        return jnp.dot(u_ref[xs * eb_sub:(xs + 1) * eb_sub, :], ht_ref[:, cols],
                       preferred_element_type=F32)

    def weighted_activation(a_t, blk):
        cols, xs = blk
        blocks = []
        for j in range(e1_per_sub):
            e1 = e * e1_per_block + xs * e1_per_sub + j
            cnt = [pltpu.bitcast(jnp.broadcast_to(cntw_ref[h, pl.ds(e1, 1), cols], (ns, ts)), BF16)
                   for h in range(heads)]
            p1 = [pltpu.bitcast(jnp.broadcast_to(p1w_ref[h, pl.ds(e1, 1), cols], (ns, ts)), BF16)
                  for h in range(heads)]
            for rg in range(n_keys // nb):
                rows = slice(rg * nb, (rg + 1) * nb)
                w = None
                for h in range(heads):
                    wh = jnp.where(rank2_ref[h, rows, cols] < cnt[h], p2_ref[h, rows, cols] * p1[h],
                                   jnp.zeros((nb, ts), BF16))
                    w = wh if w is None else w + wh
                act = _gelu(a_t[j * n_keys + rg * nb:j * n_keys + (rg + 1) * nb].astype(BF16))
                blocks.append(w * act)
        return jnp.concatenate(blocks, axis=0)

    def accumulate(wa, blk):
        cols, xs = blk
        o_ref[:, cols] += jnp.dot(vt_ref[:, xs * eb_sub:(xs + 1) * eb_sub], wa,
                                  preferred_element_type=F32)

    chains = [(slice(s * ts, (s + 1) * ts), xs) for s in range(n_tok // ts) for xs in range(n_exp_sub)]
    n_chain = len(chains)
    pre = {0: pre_activation(chains[0])}
    wa = {}
    for k in range(n_chain + 1):
        if k + 1 < n_chain:
            pre[k + 1] = pre_activation(chains[k + 1])
        if k >= 1:
            accumulate(wa.pop(k - 1), chains[k - 1])
        if k < n_chain:
            wa[k] = weighted_activation(pre.pop(k), chains[k])


def _peer_experts(ht, rank2, p2, cntw, p1w, u_bf, vt_bf, tm, eb):
    d, n = ht.shape
    heads, n_keys, _ = rank2.shape
    n_exp = u_bf.shape[0]
    tok_block = pl.BlockSpec((heads, n_keys, tm), lambda i, e: (0, 0, i))
    return pl.pallas_call(
        _peer_experts_kernel,
        grid=(n // tm, n_exp // eb),
        in_specs=[
            pl.BlockSpec((d, tm), lambda i, e: (0, i)),
            tok_block, tok_block, tok_block, tok_block,
            pl.BlockSpec((eb, d), lambda i, e: (e, 0)),
            pl.BlockSpec((d, eb), lambda i, e: (0, e)),
        ],
        out_specs=pl.BlockSpec((d, tm), lambda i, e: (0, i)),
        out_shape=jax.ShapeDtypeStruct((d, n), F32),
        compiler_params=_params("arbitrary", "arbitrary"),
        name="peer_experts",
    )(ht, rank2, p2, cntw, p1w, u_bf, vt_bf)


def _final_kernel(x_ref, pt_ref, gate_ref, g_ref, o_ref):
    x = x_ref[...] + gate_ref[0] * pt_ref[...].T
    ms = jnp.mean(x * x, axis=-1, keepdims=True)
    o_ref[...] = x * lax.rsqrt(ms + EPS) * g_ref[...]


def _final(x2, peer_t, gate, g, rows_per_vec, tm):
    n, d = x2.shape
    tiles_per_vec = rows_per_vec // tm
    return pl.pallas_call(
        _final_kernel,
        grid=(n // tm,),
        in_specs=[
            pl.BlockSpec((tm, d), lambda i: (i, 0)),
            pl.BlockSpec((d, tm), lambda i: (0, i)),
            pl.BlockSpec((1, 1, d), lambda i: (i // tiles_per_vec, 0, 0)),
            pl.BlockSpec((1, d), lambda i: (0, 0)),
        ],
        out_specs=pl.BlockSpec((tm, d), lambda i: (i, 0)),
        out_shape=jax.ShapeDtypeStruct((n, d), F32),
        compiler_params=_params("arbitrary"),
        name="final_norm",
    )(x2, peer_t, gate, g)


def kernel(x, c, ctx, c_ctx, w_mod, b_mod, norm1_g, norm2_g, w_in, conv_w, conv_b, lru_wa, lru_ba, lru_wx,
           lru_bx, lru_lambda, sgu_w, sgu_b, w_out, peer_wq, peer_k1, peer_k2, peer_u, peer_v, final_g):
    bsz, seq, d = x.shape
    ctx_len = ctx.shape[1]
    depth = w_mod.shape[0]
    assert depth == 1, "only the single-layer configuration of the reference is implemented"
    lru_width = conv_w.shape[2]
    assert seq % 512 == 0 and ctx_len % CHUNK == 0 and d % V7X_LANES == 0

    t_seq = 512
    t_tok = 512
    expert_block = 1024

    w_in_bf = w_in[0].astype(BF16)
    w_out_bf = w_out[0].astype(BF16)
    wa_bf, wx_bf = lru_wa[0].astype(BF16), lru_wx[0].astype(BF16)
    sgu_w_bf = sgu_w[0].astype(BF16)
    sgu_b_bcast = jnp.broadcast_to(sgu_b[0][:, :, None], sgu_b.shape[1:] + (sgu_w.shape[-1],))
    wq_t_bf = peer_wq[0].T.astype(BF16)
    k1_bf, k2_bf = peer_k1[0].astype(BF16), peer_k2[0].astype(BF16)
    u_bf = peer_u[0].astype(BF16)
    vt_bf = peer_v[0].T.astype(BF16)

    mod = _modulation(jnp.concatenate([c, c_ctx[None, :]], axis=0), w_mod[0], b_mod[0])
    mod = mod.reshape(bsz + 1, N_MOD, 1, d)
    lat = lambda k: mod[:bsz, k]
    ctx_vec = lambda k: mod[bsz:, k]
    g1 = norm1_g[0].reshape(1, d)
    g2 = norm2_g[0].reshape(1, d)

    px_ctx = _norm_proj(ctx.reshape(bsz * ctx_len, d), g1, ctx_vec(0), ctx_vec(1), w_in_bf[:, :lru_width],
                        rows_per_vec=bsz * ctx_len, tm=ctx_len).reshape(bsz, ctx_len, lru_width)
    proj = _norm_proj(x.reshape(bsz * seq, d), g1, lat(0), lat(1), w_in_bf,
                      rows_per_vec=seq, tm=1024).reshape(bsz, seq, w_in.shape[2])

    zeros_h = jnp.zeros((bsz, 1, lru_width), F32)
    ys = []
    for direction, reverse in enumerate((False, True)):
        prm = (conv_w[0], conv_b[0], wa_bf[direction], lru_ba[0, direction], wx_bf[direction],
               lru_bx[0, direction], lru_lambda[0, direction])
        y_ctx = _lru_scan(px_ctx, *prm, zeros_h, reverse, ctx_len)
        h_ctx = y_ctx[:, 0:1] if reverse else y_ctx[:, ctx_len - 1:ctx_len]
        ys.append(_lru_scan(proj, *prm, h_ctx, reverse, t_seq))

    x_mid = _mixer(x, proj, ys[0], ys[1], sgu_w_bf, sgu_b_bcast, w_out_bf, lat(2), t_seq)

    x2 = x_mid.reshape(bsz * seq, d)
    ht, rank2, p2, cntw, p1w = _peer_route(x2, g2, lat(3), lat(4), wq_t_bf, k1_bf, k2_bf,
                                           rows_per_vec=seq, tm=t_tok)
    peer_t = _peer_experts(ht, rank2, p2, cntw, p1w, u_bf, vt_bf, t_tok, expert_block)
    out = _final(x2, peer_t, lat(5), final_g.reshape(1, d), rows_per_vec=seq, tm=512)
    return out.reshape(bsz, seq, d)
```

```python
import functools

import jax
import jax.numpy as jnp
from jax import lax
from jax.experimental import pallas as pl
from jax.experimental.pallas import tpu as pltpu

F32 = jnp.float32
BF16 = jnp.bfloat16
U32 = jnp.uint32

EPS = 1e-6
LRU_C = 8.0
CONV_LEFT = 2
CHUNK = 128
PEER_TOPK = 16
N_MOD = 6

V7X_LANES = 128
V7X_SUBLANES = 8
V7X_BF16_ROWS = 16
V7X_MXU_COLS = 256
V7X_VMEM_LIMIT_BYTES = 56 * 1024 * 1024


def _params(*semantics):
    return pltpu.CompilerParams(dimension_semantics=semantics, vmem_limit_bytes=V7X_VMEM_LIMIT_BYTES)


def _gelu(x):
    return 0.5 * x * (1.0 + jnp.tanh(0.7978845608028654 * (x + 0.044715 * (x * x * x))))


def _norm_modulate(x, g, shift, scale):
    ms = jnp.mean(x * x, axis=-1, keepdims=True)
    return (x * lax.rsqrt(ms + EPS) * g) * (1.0 + scale) + shift


def _modulation_kernel(cs_ref, w_ref, b_ref, o_ref):
    n_rows = cs_ref.shape[0]
    nb = w_ref.shape[1]
    rows = []
    for r in range(n_rows):
        c = cs_ref[r]
        s = c * jax.nn.sigmoid(c)
        parts = [jnp.sum(w_ref[:, n * V7X_LANES:(n + 1) * V7X_LANES] * s, axis=0, keepdims=True)
                 for n in range(nb // V7X_LANES)]
        rows.append(jnp.concatenate(parts, axis=1) + b_ref[...])
    rows.append(jnp.zeros((V7X_SUBLANES - n_rows, nb), F32))
    o_ref[...] = jnp.concatenate(rows, axis=0)


def _modulation(cs, w_mod, b_mod):
    n_rows, d = cs.shape
    n = w_mod.shape[1]
    nb = 1024
    csb = jnp.broadcast_to(cs[:, :, None], (n_rows, d, V7X_LANES))
    out = pl.pallas_call(
        _modulation_kernel,
        grid=(n // nb,),
        in_specs=[
            pl.BlockSpec((n_rows, d, V7X_LANES), lambda j: (0, 0, 0)),
            pl.BlockSpec((d, nb), lambda j: (0, j)),
            pl.BlockSpec((1, nb), lambda j: (0, j)),
        ],
        out_specs=pl.BlockSpec((V7X_SUBLANES, nb), lambda j: (0, j)),
        out_shape=jax.ShapeDtypeStruct((V7X_SUBLANES, n), F32),
        compiler_params=_params("arbitrary"),
        name="modulation",
    )(csb, w_mod, b_mod.reshape(1, n))
    return out[:n_rows]


def _norm_proj_kernel(x_ref, g_ref, shift_ref, scale_ref, w_ref, o_ref):
    hn = _norm_modulate(x_ref[...], g_ref[...], shift_ref[0], scale_ref[0])
    o_ref[...] = jnp.dot(hn.astype(BF16), w_ref[...], preferred_element_type=F32)


def _norm_proj(x2, g, shift, scale, w, rows_per_vec, tm):
    r, d = x2.shape
    n = w.shape[1]
    tn = 1024
    tiles_per_vec = rows_per_vec // tm
    return pl.pallas_call(
        _norm_proj_kernel,
        grid=(r // tm, n // tn),
        in_specs=[
            pl.BlockSpec((tm, d), lambda i, j: (i, 0)),
            pl.BlockSpec((1, d), lambda i, j: (0, 0)),
            pl.BlockSpec((1, 1, d), lambda i, j: (i // tiles_per_vec, 0, 0)),
            pl.BlockSpec((1, 1, d), lambda i, j: (i // tiles_per_vec, 0, 0)),
            pl.BlockSpec((d, tn), lambda i, j: (0, j)),
        ],
        out_specs=pl.BlockSpec((tm, tn), lambda i, j: (i, j)),
        out_shape=jax.ShapeDtypeStruct((r, n), F32),
        compiler_params=_params("arbitrary", "arbitrary"),
        name="norm_proj",
    )(x2, g, shift, scale, w)


def _lru_scan_kernel(cur_ref, prev_ref, next_ref, cw_ref, cb_ref, wa_ref, ba_ref, wx_ref, bx_ref,
                     lam_ref, h0_ref, y_ref, xe_ref, a_ref, b_ref, h_ref, *, reverse, n_tiles):
    i = pl.program_id(1)
    ti = (n_tiles - 1 - i) if reverse else i
    t_rows, width = a_ref.shape
    heads, head_dim = wa_ref.shape[0], wa_ref.shape[1]
    halo = V7X_SUBLANES

    @pl.when(i == 0)
    def _():
        h_ref[...] = h0_ref[0]

    zero = jnp.zeros((halo, width), F32)
    xe_ref[0:halo, :] = jnp.where(ti == 0, zero, prev_ref[0])
    xe_ref[halo:halo + t_rows, :] = cur_ref[0]
    xe_ref[halo + t_rows:, :] = jnp.where(ti == n_tiles - 1, zero, next_ref[0])

    xc = jnp.broadcast_to(cb_ref[...], (t_rows, width))
    for k in range(cw_ref.shape[0]):
        lo = halo - CONV_LEFT + k
        xc = xc + xe_ref[lo:lo + t_rows, :] * cw_ref[k:k + 1, :]

    xcb = xc.astype(BF16)
    r_parts, i_parts = [], []
    for h in range(heads):
        xh = xcb[:, h * head_dim:(h + 1) * head_dim]
        r_parts.append(jnp.dot(xh, wa_ref[h], preferred_element_type=F32))
        i_parts.append(jnp.dot(xh, wx_ref[h], preferred_element_type=F32))
    r = jax.nn.sigmoid(jnp.concatenate(r_parts, axis=1) + ba_ref[...])
    ig = jax.nn.sigmoid(jnp.concatenate(i_parts, axis=1) + bx_ref[...])
    lam = lam_ref[...]
    softplus_neg_lam = jnp.maximum(-lam, 0.0) + jnp.log1p(jnp.exp(-jnp.abs(lam)))
    log_a = -LRU_C * r * softplus_neg_lam
    a_ref[...] = jnp.exp(log_a)
    th = jnp.tanh(log_a)
    b_ref[...] = jnp.sqrt(-2.0 * th / (1.0 - th)) * (ig * xc)

    def step(s, h):
        t = (t_rows - 1 - s) if reverse else s
        h = a_ref[pl.ds(t, 1), :] * h + b_ref[pl.ds(t, 1), :]
        y_ref[0, pl.ds(t, 1), :] = h
        return h

    h_ref[...] = lax.fori_loop(0, t_rows, step, h_ref[...], unroll=8)


def _lru_scan(px, conv_w, conv_b, wa, ba, wx, bx, lam, h0, reverse, t_rows):
    bsz, seq, _ = px.shape
    width = conv_w.shape[1]
    n_tiles = seq // t_rows
    halo = V7X_SUBLANES
    tb = t_rows // halo
    last_halo_block = seq // halo - 1

    def tile(i):
        return (n_tiles - 1 - i) if reverse else i

    vec = lambda b, i: (0, 0)
    kern = functools.partial(_lru_scan_kernel, reverse=reverse, n_tiles=n_tiles)
    return pl.pallas_call(
        kern,
        grid=(bsz, n_tiles),
        in_specs=[
            pl.BlockSpec((1, t_rows, width), lambda b, i: (b, tile(i), 0)),
            pl.BlockSpec((1, halo, width), lambda b, i: (b, jnp.maximum(tile(i) * tb - 1, 0), 0)),
            pl.BlockSpec((1, halo, width), lambda b, i: (b, jnp.minimum((tile(i) + 1) * tb, last_halo_block), 0)),
            pl.BlockSpec(conv_w.shape, vec),
            pl.BlockSpec((1, width), vec),
            pl.BlockSpec(wa.shape, lambda b, i: (0, 0, 0)),
            pl.BlockSpec((1, width), vec),
            pl.BlockSpec(wx.shape, lambda b, i: (0, 0, 0)),
            pl.BlockSpec((1, width), vec),
            pl.BlockSpec((1, width), vec),
            pl.BlockSpec((1, 1, width), lambda b, i: (b, 0, 0)),
        ],
        out_specs=pl.BlockSpec((1, t_rows, width), lambda b, i: (b, tile(i), 0)),
        out_shape=jax.ShapeDtypeStruct((bsz, seq, width), F32),
        scratch_shapes=[
            pltpu.VMEM((t_rows + 2 * halo, width), F32),
            pltpu.VMEM((t_rows, width), F32),
            pltpu.VMEM((t_rows, width), F32),
            pltpu.VMEM((1, width), F32),
        ],
        compiler_params=_params("arbitrary", "arbitrary"),
        name="lru_scan_bwd" if reverse else "lru_scan_fwd",
    )(px, px, px, conv_w, conv_b.reshape(1, width), wa, ba.reshape(1, width), wx, bx.reshape(1, width),
      lam.reshape(1, width), h0)


def _mixer_kernel(x_ref, gate_ref, u_ref, v_ref, yf_ref, yb_ref, sw_ref, sb_ref, wout_ref, g1_ref, o_ref,
                  ycat_ref):
    t_rows = x_ref.shape[1]
    lru_width = gate_ref.shape[2]
    groups, group_dim = sw_ref.shape[0], u_ref.shape[2] // sw_ref.shape[0]

    ycat_ref[:, 0:lru_width] = (_gelu(gate_ref[0]) * (yf_ref[0] + yb_ref[0])).astype(BF16)
    y_lru = jnp.dot(ycat_ref[:, 0:lru_width], wout_ref[0:lru_width, :], preferred_element_type=F32)
    for n in range(t_rows // CHUNK):
        rows = slice(n * CHUNK, (n + 1) * CHUNK)
        for g in range(groups):
            cols = slice(g * group_dim, (g + 1) * group_dim)
            vg = _gelu(v_ref[0, rows, cols])
            mu = jnp.mean(vg, axis=-1, keepdims=True)
            dv = vg - mu
            var = jnp.mean(dv * dv, axis=-1, keepdims=True)
            vn = dv * lax.rsqrt(var + EPS)
            mixed = jnp.dot(sw_ref[g], vn.astype(BF16), preferred_element_type=F32) + sb_ref[g]
            ycat_ref[rows, lru_width + g * group_dim:lru_width + (g + 1) * group_dim] = (
                _gelu(u_ref[0, rows, cols]) * mixed).astype(BF16)
    y = y_lru + jnp.dot(ycat_ref[:, lru_width:], wout_ref[lru_width:, :], preferred_element_type=F32)
    o_ref[0] = x_ref[0] + g1_ref[0] * y


def _mixer(x, proj, y_fwd, y_bwd, sgu_w, sgu_b_bcast, w_out, gate1, t_rows):
    bsz, seq, d = x.shape
    width = y_fwd.shape[2]
    tile3 = lambda c: pl.BlockSpec((1, t_rows, width), lambda b, i: (b, i, c))
    return pl.pallas_call(
        _mixer_kernel,
        grid=(bsz, seq // t_rows),
        in_specs=[
            pl.BlockSpec((1, t_rows, d), lambda b, i: (b, i, 0)),
            tile3(1), tile3(2), tile3(3),
            tile3(0), tile3(0),
            pl.BlockSpec(sgu_w.shape, lambda b, i: (0, 0, 0)),
            pl.BlockSpec(sgu_b_bcast.shape, lambda b, i: (0, 0, 0)),
            pl.BlockSpec(w_out.shape, lambda b, i: (0, 0)),
            pl.BlockSpec((1, 1, d), lambda b, i: (b, 0, 0)),
        ],
        out_specs=pl.BlockSpec((1, t_rows, d), lambda b, i: (b, i, 0)),
        out_shape=jax.ShapeDtypeStruct((bsz, seq, d), F32),
        scratch_shapes=[pltpu.VMEM((t_rows, w_out.shape[0]), BF16)],
        compiler_params=_params("arbitrary", "arbitrary"),
        name="mixer",
    )(x, proj, proj, proj, y_fwd, y_bwd, sgu_w, sgu_b_bcast, w_out, gate1)


def _oddeven_mergesort_pairs(n):
    pairs = []
    p = 1
    while p < n:
        k = p
        while k >= 1:
            for j in range(k % p, n - k, 2 * k):
                for i in range(min(k, n - j - k)):
                    if (i + j) // (2 * p) == (i + j + k) // (2 * p):
                        pairs.append((i + j, i + j + k))
            k //= 2
        p *= 2
    return pairs


def _bitonic_merge_pairs(n):
    pairs = []
    d = n // 2
    while d >= 1:
        pairs.extend((i, i + d) for i in range(n) if not i & d)
        d //= 2
    return pairs


def _compare_exchange(v, pairs):
    for i, j in pairs:
        a, b = v[i], v[j]
        if b is None:
            continue
        if a is None:
            v[i], v[j] = b, None
        else:
            v[i], v[j] = jnp.maximum(a, b), jnp.minimum(a, b)


def _top16_groups(groups):
    v = list(groups)
    n = len(v)
    _compare_exchange(v, _oddeven_mergesort_pairs(n))
    shift = V7X_SUBLANES // 2
    while shift >= 1:
        merged = []
        for i in range(n):
            a, b = v[i], v[n - 1 - i]
            b = None if b is None else pltpu.roll(b, shift, 0)
            merged.append(b if a is None else (a if b is None else jnp.maximum(a, b)))
        v = merged
        _compare_exchange(v, _bitonic_merge_pairs(n))
        shift //= 2
    return v


def _sublane_gather(groups, sub):
    out = groups[-1]
    for j in range(len(groups) - 2, -1, -1):
        out = jnp.where(sub == j, groups[j], out)
    return out


def _pair_candidates(v1, v2, sub):
    ns = V7X_SUBLANES
    neg_inf = jnp.float32(-jnp.inf)
    v2_lo = _sublane_gather(v2[:ns], sub)
    v2_hi = _sublane_gather(v2[ns:], sub)
    v1_hi = _sublane_gather(v1[ns:], sub)
    groups = [v1[0] + v2_lo, v1[0] + v2_hi, v1[1] + v2_lo]
    for i in range(2, ns):
        groups.append(jnp.where(sub < PEER_TOPK // (i + 1), v1[i] + v2_lo, neg_inf))
    groups.append(v1_hi + v2[0])
    return groups


def _monotone_count(v, pred):
    w = jnp.where
    c1 = pred(v[7])
    c2 = pred(w(c1, v[11], v[3]))
    c3 = pred(w(c1, w(c2, v[13], v[9]), w(c2, v[5], v[1])))
    c4 = pred(w(c1, w(c2, w(c3, v[14], v[12]), w(c3, v[10], v[8])),
                w(c2, w(c3, v[6], v[4]), w(c3, v[2], v[0]))))
    c5 = pred(v[15])
    return (w(c1, 8.0, 0.0) + w(c2, 4.0, 0.0)) + (w(c3, 2.0, 0.0) + w(c4, 1.0, 0.0)) + w(c5, 1.0, 0.0)


def _bf16_pair_words(x):
    bits = pltpu.bitcast(x.astype(BF16).astype(F32), U32)
    return bits | lax.shift_right_logical(bits, jnp.uint32(16))


def _peer_route_kernel(x_ref, g_ref, shift_ref, scale_ref, wqt_ref, k1_ref, k2_ref,
                       ht_ref, rank2_ref, p2_ref, cntw_ref, p1w_ref):
    heads, n_keys, half = k1_ref.shape
    n_tok = x_ref.shape[0]
    ns = V7X_SUBLANES
    n_groups = n_keys // ns
    hn = _norm_modulate(x_ref[...], g_ref[...], shift_ref[0], scale_ref[0])
    ht = hn.T.astype(BF16)
    ht_ref[...] = ht
    qt = jnp.dot(wqt_ref[...], ht, preferred_element_type=F32)
    lanes = V7X_LANES
    sub = lax.broadcasted_iota(jnp.int32, (ns, lanes), 0)
    for h in range(heads):
        q1 = qt[(2 * h) * half:(2 * h + 1) * half].astype(BF16)
        q2 = qt[(2 * h + 1) * half:(2 * h + 2) * half].astype(BF16)
        s1_all = jnp.dot(k1_ref[h], q1, preferred_element_type=F32)
        s2_all = jnp.dot(k2_ref[h], q2, preferred_element_type=F32)
        for c in range(n_tok // lanes):
            cols = slice(c * lanes, (c + 1) * lanes)
            s1, s2 = s1_all[:, cols], s2_all[:, cols]
            s1g = s1.reshape(n_groups, ns, lanes)
            s2g = s2.reshape(n_groups, ns, lanes)
            v1 = _top16_groups([s1g[i] for i in range(n_groups)])
            v2 = _top16_groups([s2g[i] for i in range(n_groups)])
            cand = _pair_candidates(v1, v2, sub)
            tau = _top16_groups(cand + [None] * (n_groups - len(cand)))[PEER_TOPK - 1]
            m = v1[0] + v2[0]
            zs = None
            for cg in cand:
                term = jnp.where(cg >= tau, jnp.exp(cg - m), 0.0)
                zs = term if zs is None else zs + term
            inv_z = 1.0 / jnp.sum(zs, axis=0, keepdims=True)
            v2b = [v[None] for v in v2]
            rank2 = _monotone_count(v2b, lambda t: t > s2g)
            cnt = _monotone_count(v2b, lambda t: s1g + t >= tau[None])
            rank2_ref[h, :, cols] = rank2.reshape(n_keys, lanes).astype(BF16)
            p2_ref[h, :, cols] = jnp.exp(s2 - v2[0][0:1]).astype(BF16)
            cntw_ref[h, :, cols] = _bf16_pair_words(cnt.reshape(n_keys, lanes))
            p1w_ref[h, :, cols] = _bf16_pair_words(jnp.exp(s1 - v1[0][0:1]) * inv_z)


def _peer_route(x2, g, shift, scale, wq_t, k1, k2, rows_per_vec, tm):
    n, d = x2.shape
    heads, n_keys, _ = k1.shape
    tiles_per_vec = rows_per_vec // tm
    tok_block = pl.BlockSpec((heads, n_keys, tm), lambda i: (0, 0, i))
    return pl.pallas_call(
        _peer_route_kernel,
        grid=(n // tm,),
        in_specs=[
            pl.BlockSpec((tm, d), lambda i: (i, 0)),
            pl.BlockSpec((1, d), lambda i: (0, 0)),
            pl.BlockSpec((1, 1, d), lambda i: (i // tiles_per_vec, 0, 0)),
            pl.BlockSpec((1, 1, d), lambda i: (i // tiles_per_vec, 0, 0)),
            pl.BlockSpec(wq_t.shape, lambda i: (0, 0)),
            pl.BlockSpec(k1.shape, lambda i: (0, 0, 0)),
            pl.BlockSpec(k2.shape, lambda i: (0, 0, 0)),
        ],
        out_specs=[pl.BlockSpec((d, tm), lambda i: (0, i)), tok_block, tok_block, tok_block, tok_block],
        out_shape=[
            jax.ShapeDtypeStruct((d, n), BF16),
            jax.ShapeDtypeStruct((heads, n_keys, n), BF16),
            jax.ShapeDtypeStruct((heads, n_keys, n), BF16),
            jax.ShapeDtypeStruct((heads, n_keys, n), U32),
            jax.ShapeDtypeStruct((heads, n_keys, n), U32),
        ],
        compiler_params=_params("arbitrary"),
        name="peer_route",
    )(x2, g, shift, scale, wq_t, k1, k2)


def _peer_experts_kernel(ht_ref, rank2_ref, p2_ref, cntw_ref, p1w_ref, u_ref, vt_ref, o_ref):
    e = pl.program_id(1)
    heads, n_keys, n_tok = rank2_ref.shape
    e1_per_block = u_ref.shape[0] // n_keys
    ts = V7X_MXU_COLS
    nb = V7X_BF16_ROWS
    ns = V7X_SUBLANES

    @pl.when(e == 0)
    def _():
        o_ref[...] = jnp.zeros_like(o_ref)

    e1_splits = (0, e1_per_block // 2, e1_per_block)

    def pre_activation(blk):
        cols, lo, hi = blk
        return jnp.dot(u_ref[lo * n_keys:hi * n_keys, :], ht_ref[:, cols],
                       preferred_element_type=F32)

    def weighted_activation(a_t, blk):
        cols, lo, hi = blk
        blocks = []
        for j in range(hi - lo):
            e1 = e * e1_per_block + lo + j
            cnt = [pltpu.bitcast(jnp.broadcast_to(cntw_ref[h, pl.ds(e1, 1), cols], (ns, ts)), BF16)
                   for h in range(heads)]
            p1 = [pltpu.bitcast(jnp.broadcast_to(p1w_ref[h, pl.ds(e1, 1), cols], (ns, ts)), BF16)
                  for h in range(heads)]
            for rg in range(n_keys // nb):
                rows = slice(rg * nb, (rg + 1) * nb)
                w = None
                for h in range(heads):
                    wh = jnp.where(rank2_ref[h, rows, cols] < cnt[h], p2_ref[h, rows, cols] * p1[h],
                                   jnp.zeros((nb, ts), BF16))
                    w = wh if w is None else w + wh
                act = _gelu(a_t[j * n_keys + rg * nb:j * n_keys + (rg + 1) * nb].astype(BF16))
                blocks.append(w * act)
        return jnp.concatenate(blocks, axis=0)

    def accumulate(wa, blk):
        cols, lo, hi = blk
        o_ref[:, cols] += jnp.dot(vt_ref[:, lo * n_keys:hi * n_keys], wa,
                                  preferred_element_type=F32)

    chains = [(slice(s * ts, (s + 1) * ts), lo, hi) for s in range(n_tok // ts)
              for lo, hi in zip(e1_splits[:-1], e1_splits[1:])]
    n_chain = len(chains)
    pre = {0: pre_activation(chains[0])}
    wa = {}
    for k in range(n_chain + 1):
        if k + 1 < n_chain:
            pre[k + 1] = pre_activation(chains[k + 1])
        if k >= 1:
            accumulate(wa.pop(k - 1), chains[k - 1])
        if k < n_chain:
            wa[k] = weighted_activation(pre.pop(k), chains[k])


def _peer_experts(ht, rank2, p2, cntw, p1w, u_bf, vt_bf, tm, eb):
    d, n = ht.shape
    heads, n_keys, _ = rank2.shape
    n_exp = u_bf.shape[0]
    tok_block = pl.BlockSpec((heads, n_keys, tm), lambda i, e: (0, 0, i))
    return pl.pallas_call(
        _peer_experts_kernel,
        grid=(n // tm, n_exp // eb),
        in_specs=[
            pl.BlockSpec((d, tm), lambda i, e: (0, i)),
            tok_block, tok_block, tok_block, tok_block,
            pl.BlockSpec((eb, d), lambda i, e: (e, 0)),
            pl.BlockSpec((d, eb), lambda i, e: (0, e)),
        ],
        out_specs=pl.BlockSpec((d, tm), lambda i, e: (0, i)),
        out_shape=jax.ShapeDtypeStruct((d, n), F32),
        compiler_params=_params("arbitrary", "arbitrary"),
        name="peer_experts",
    )(ht, rank2, p2, cntw, p1w, u_bf, vt_bf)


def _final_kernel(x_ref, pt_ref, gate_ref, g_ref, o_ref):
    x = x_ref[...] + gate_ref[0] * pt_ref[...].T
    ms = jnp.mean(x * x, axis=-1, keepdims=True)
    o_ref[...] = x * lax.rsqrt(ms + EPS) * g_ref[...]


def _final(x2, peer_t, gate, g, rows_per_vec, tm):
    n, d = x2.shape
    tiles_per_vec = rows_per_vec // tm
    return pl.pallas_call(
        _final_kernel,
        grid=(n // tm,),
        in_specs=[
            pl.BlockSpec((tm, d), lambda i: (i, 0)),
            pl.BlockSpec((d, tm), lambda i: (0, i)),
            pl.BlockSpec((1, 1, d), lambda i: (i // tiles_per_vec, 0, 0)),
            pl.BlockSpec((1, d), lambda i: (0, 0)),
        ],
        out_specs=pl.BlockSpec((tm, d), lambda i: (i, 0)),
        out_shape=jax.ShapeDtypeStruct((n, d), F32),
        compiler_params=_params("arbitrary"),
        name="final_norm",
    )(x2, peer_t, gate, g)


def kernel(x, c, ctx, c_ctx, w_mod, b_mod, norm1_g, norm2_g, w_in, conv_w, conv_b, lru_wa, lru_ba, lru_wx,
           lru_bx, lru_lambda, sgu_w, sgu_b, w_out, peer_wq, peer_k1, peer_k2, peer_u, peer_v, final_g):
    bsz, seq, d = x.shape
    ctx_len = ctx.shape[1]
    depth = w_mod.shape[0]
    assert depth == 1, "only the single-layer configuration of the reference is implemented"
    lru_width = conv_w.shape[2]
    assert seq % 512 == 0 and ctx_len % CHUNK == 0 and d % V7X_LANES == 0

    t_seq = 512
    t_tok = 512
    expert_block = 1024

    w_in_bf = w_in[0].astype(BF16)
    w_out_bf = w_out[0].astype(BF16)
    wa_bf, wx_bf = lru_wa[0].astype(BF16), lru_wx[0].astype(BF16)
    sgu_w_bf = sgu_w[0].astype(BF16)
    sgu_b_bcast = jnp.broadcast_to(sgu_b[0][:, :, None], sgu_b.shape[1:] + (sgu_w.shape[-1],))
    wq_t_bf = peer_wq[0].T.astype(BF16)
    k1_bf, k2_bf = peer_k1[0].astype(BF16), peer_k2[0].astype(BF16)
    u_bf = peer_u[0].astype(BF16)
    vt_bf = peer_v[0].T.astype(BF16)

    mod = _modulation(jnp.concatenate([c, c_ctx[None, :]], axis=0), w_mod[0], b_mod[0])
    mod = mod.reshape(bsz + 1, N_MOD, 1, d)
    lat = lambda k: mod[:bsz, k]
    ctx_vec = lambda k: mod[bsz:, k]
    g1 = norm1_g[0].reshape(1, d)
    g2 = norm2_g[0].reshape(1, d)

    px_ctx = _norm_proj(ctx.reshape(bsz * ctx_len, d), g1, ctx_vec(0), ctx_vec(1), w_in_bf[:, :lru_width],
                        rows_per_vec=bsz * ctx_len, tm=ctx_len).reshape(bsz, ctx_len, lru_width)
    proj = _norm_proj(x.reshape(bsz * seq, d), g1, lat(0), lat(1), w_in_bf,
                      rows_per_vec=seq, tm=1024).reshape(bsz, seq, w_in.shape[2])

    zeros_h = jnp.zeros((bsz, 1, lru_width), F32)
    ys = []
    for direction, reverse in enumerate((False, True)):
        prm = (conv_w[0], conv_b[0], wa_bf[direction], lru_ba[0, direction], wx_bf[direction],
               lru_bx[0, direction], lru_lambda[0, direction])
        y_ctx = _lru_scan(px_ctx, *prm, zeros_h, reverse, ctx_len)
        h_ctx = y_ctx[:, 0:1] if reverse else y_ctx[:, ctx_len - 1:ctx_len]
        ys.append(_lru_scan(proj, *prm, h_ctx, reverse, t_seq))

    x_mid = _mixer(x, proj, ys[0], ys[1], sgu_w_bf, sgu_b_bcast, w_out_bf, lat(2), t_seq)

    x2 = x_mid.reshape(bsz * seq, d)
    ht, rank2, p2, cntw, p1w = _peer_route(x2, g2, lat(3), lat(4), wq_t_bf, k1_bf, k2_bf,
                                           rows_per_vec=seq, tm=t_tok)
    peer_t = _peer_experts(ht, rank2, p2, cntw, p1w, u_bf, vt_bf, t_tok, expert_block)
    out = _final(x2, peer_t, lat(5), final_g.reshape(1, d), rows_per_vec=seq, tm=512)
    return out.reshape(bsz, seq, d)
```

```python
import functools

import jax
import jax.numpy as jnp
from jax import lax
from jax.experimental import pallas as pl
from jax.experimental.pallas import tpu as pltpu

F32 = jnp.float32
BF16 = jnp.bfloat16
U32 = jnp.uint32

EPS = 1e-6
LRU_C = 8.0
CONV_LEFT = 2
CHUNK = 128
PEER_TOPK = 16
N_MOD = 6

V7X_LANES = 128
V7X_SUBLANES = 8
V7X_BF16_ROWS = 16
V7X_MXU_COLS = 256
V7X_VMEM_LIMIT_BYTES = 56 * 1024 * 1024


def _params(*semantics):
    return pltpu.CompilerParams(dimension_semantics=semantics, vmem_limit_bytes=V7X_VMEM_LIMIT_BYTES)


def _gelu(x):
    return 0.5 * x * (1.0 + jnp.tanh(0.7978845608028654 * (x + 0.044715 * (x * x * x))))


def _norm_modulate(x, g, shift, scale):
    ms = jnp.mean(x * x, axis=-1, keepdims=True)
    return (x * lax.rsqrt(ms + EPS) * g) * (1.0 + scale) + shift


def _modulation_kernel(cs_ref, w_ref, b_ref, o_ref):
    n_rows = cs_ref.shape[0]
    nb = w_ref.shape[1]
    rows = []
    for r in range(n_rows):
        c = cs_ref[r]
        s = c * jax.nn.sigmoid(c)
        parts = [jnp.sum(w_ref[:, n * V7X_LANES:(n + 1) * V7X_LANES] * s, axis=0, keepdims=True)
                 for n in range(nb // V7X_LANES)]
        rows.append(jnp.concatenate(parts, axis=1) + b_ref[...])
    rows.append(jnp.zeros((V7X_SUBLANES - n_rows, nb), F32))
    o_ref[...] = jnp.concatenate(rows, axis=0)


def _modulation(cs, w_mod, b_mod):
    n_rows, d = cs.shape
    n = w_mod.shape[1]
    nb = 1024
    csb = jnp.broadcast_to(cs[:, :, None], (n_rows, d, V7X_LANES))
    out = pl.pallas_call(
        _modulation_kernel,
        grid=(n // nb,),
        in_specs=[
            pl.BlockSpec((n_rows, d, V7X_LANES), lambda j: (0, 0, 0)),
            pl.BlockSpec((d, nb), lambda j: (0, j)),
            pl.BlockSpec((1, nb), lambda j: (0, j)),
        ],
        out_specs=pl.BlockSpec((V7X_SUBLANES, nb), lambda j: (0, j)),
        out_shape=jax.ShapeDtypeStruct((V7X_SUBLANES, n), F32),
        compiler_params=_params("arbitrary"),
        name="modulation",
    )(csb, w_mod, b_mod.reshape(1, n))
    return out[:n_rows]


def _norm_proj_kernel(x_ref, g_ref, shift_ref, scale_ref, w_ref, o_ref):
    hn = _norm_modulate(x_ref[...], g_ref[...], shift_ref[0], scale_ref[0])
    o_ref[...] = jnp.dot(hn.astype(BF16), w_ref[...], preferred_element_type=F32)


def _norm_proj(x2, g, shift, scale, w, rows_per_vec, tm):
    r, d = x2.shape
    n = w.shape[1]
    tn = min(n, 2048)
    tiles_per_vec = rows_per_vec // tm
    return pl.pallas_call(
        _norm_proj_kernel,
        grid=(r // tm, n // tn),
        in_specs=[
            pl.BlockSpec((tm, d), lambda i, j: (i, 0)),
            pl.BlockSpec((1, d), lambda i, j: (0, 0)),
            pl.BlockSpec((1, 1, d), lambda i, j: (i // tiles_per_vec, 0, 0)),
            pl.BlockSpec((1, 1, d), lambda i, j: (i // tiles_per_vec, 0, 0)),
            pl.BlockSpec((d, tn), lambda i, j: (0, j)),
        ],
        out_specs=pl.BlockSpec((tm, tn), lambda i, j: (i, j)),
        out_shape=jax.ShapeDtypeStruct((r, n), F32),
        compiler_params=_params("arbitrary", "arbitrary"),
        name="norm_proj",
    )(x2, g, shift, scale, w)


def _lru_scan_kernel(cur_ref, prev_ref, next_ref, cw_ref, cb_ref, wa_ref, ba_ref, wx_ref, bx_ref,
                     lam_ref, h0_ref, y_ref, xe_ref, a_ref, b_ref, h_ref, *, reverse, n_tiles):
    i = pl.program_id(1)
    ti = (n_tiles - 1 - i) if reverse else i
    t_rows, width = a_ref.shape
    heads, head_dim = wa_ref.shape[0], wa_ref.shape[1]
    halo = V7X_SUBLANES

    @pl.when(i == 0)
    def _():
        h_ref[...] = h0_ref[0]

    zero = jnp.zeros((halo, width), F32)
    xe_ref[0:halo, :] = jnp.where(ti == 0, zero, prev_ref[0])
    xe_ref[halo:halo + t_rows, :] = cur_ref[0]
    xe_ref[halo + t_rows:, :] = jnp.where(ti == n_tiles - 1, zero, next_ref[0])

    xc = jnp.broadcast_to(cb_ref[...], (t_rows, width))
    for k in range(cw_ref.shape[0]):
        lo = halo - CONV_LEFT + k
        xc = xc + xe_ref[lo:lo + t_rows, :] * cw_ref[k:k + 1, :]

    xcb = xc.astype(BF16)
    r_parts, i_parts = [], []
    for h in range(heads):
        xh = xcb[:, h * head_dim:(h + 1) * head_dim]
        r_parts.append(jnp.dot(xh, wa_ref[h], preferred_element_type=F32))
        i_parts.append(jnp.dot(xh, wx_ref[h], preferred_element_type=F32))
    r = 0.5 * (1.0 + jnp.tanh(0.5 * (jnp.concatenate(r_parts, axis=1) + ba_ref[...])))
    ig = 0.5 * (1.0 + jnp.tanh(0.5 * (jnp.concatenate(i_parts, axis=1) + bx_ref[...])))
    lam = lam_ref[...]
    softplus_neg_lam = jnp.maximum(-lam, 0.0) + jnp.log1p(jnp.exp(-jnp.abs(lam)))
    log_a = -LRU_C * r * softplus_neg_lam
    a_ref[...] = jnp.exp(log_a)
    th = jnp.tanh(log_a)
    b_ref[...] = (jnp.sqrt(-2.0 * th) * lax.rsqrt(1.0 - th)) * (ig * xc)

    def step(s, h):
        t = (t_rows - 1 - s) if reverse else s
        h = a_ref[pl.ds(t, 1), :] * h + b_ref[pl.ds(t, 1), :]
        y_ref[0, pl.ds(t, 1), :] = h
        return h

    h_ref[...] = lax.fori_loop(0, t_rows, step, h_ref[...], unroll=8)


def _lru_scan(px, conv_w, conv_b, wa, ba, wx, bx, lam, h0, reverse, t_rows):
    bsz, seq, _ = px.shape
    width = conv_w.shape[1]
    n_tiles = seq // t_rows
    halo = V7X_SUBLANES
    tb = t_rows // halo
    last_halo_block = seq // halo - 1

    def tile(i):
        return (n_tiles - 1 - i) if reverse else i

    vec = lambda b, i: (0, 0)
    kern = functools.partial(_lru_scan_kernel, reverse=reverse, n_tiles=n_tiles)
    return pl.pallas_call(
        kern,
        grid=(bsz, n_tiles),
        in_specs=[
            pl.BlockSpec((1, t_rows, width), lambda b, i: (b, tile(i), 0)),
            pl.BlockSpec((1, halo, width), lambda b, i: (b, jnp.maximum(tile(i) * tb - 1, 0), 0)),
            pl.BlockSpec((1, halo, width), lambda b, i: (b, jnp.minimum((tile(i) + 1) * tb, last_halo_block), 0)),
            pl.BlockSpec(conv_w.shape, vec),
            pl.BlockSpec((1, width), vec),
            pl.BlockSpec(wa.shape, lambda b, i: (0, 0, 0)),
            pl.BlockSpec((1, width), vec),
            pl.BlockSpec(wx.shape, lambda b, i: (0, 0, 0)),
            pl.BlockSpec((1, width), vec),
            pl.BlockSpec((1, width), vec),
            pl.BlockSpec((1, 1, width), lambda b, i: (b, 0, 0)),
        ],
        out_specs=pl.BlockSpec((1, t_rows, width), lambda b, i: (b, tile(i), 0)),
        out_shape=jax.ShapeDtypeStruct((bsz, seq, width), F32),
        scratch_shapes=[
            pltpu.VMEM((t_rows + 2 * halo, width), F32),
            pltpu.VMEM((t_rows, width), F32),
            pltpu.VMEM((t_rows, width), F32),
            pltpu.VMEM((1, width), F32),
        ],
        compiler_params=_params("arbitrary", "arbitrary"),
        name="lru_scan_bwd" if reverse else "lru_scan_fwd",
    )(px, px, px, conv_w, conv_b.reshape(1, width), wa, ba.reshape(1, width), wx, bx.reshape(1, width),
      lam.reshape(1, width), h0)


def _mixer_kernel(x_ref, gate_ref, u_ref, v_ref, yf_ref, yb_ref, sw_ref, sb_ref, wout_ref, g1_ref, o_ref,
                  ycat_ref):
    t_rows = x_ref.shape[1]
    lru_width = gate_ref.shape[2]
    groups, group_dim = sw_ref.shape[0], u_ref.shape[2] // sw_ref.shape[0]

    ycat_ref[:, 0:lru_width] = (_gelu(gate_ref[0]) * (yf_ref[0] + yb_ref[0])).astype(BF16)
    y_lru = jnp.dot(ycat_ref[:, 0:lru_width], wout_ref[0:lru_width, :], preferred_element_type=F32)
    for n in range(t_rows // CHUNK):
        rows = slice(n * CHUNK, (n + 1) * CHUNK)
        for g in range(groups):
            cols = slice(g * group_dim, (g + 1) * group_dim)
            vg = _gelu(v_ref[0, rows, cols])
            mu = jnp.mean(vg, axis=-1, keepdims=True)
            dv = vg - mu
            var = jnp.mean(dv * dv, axis=-1, keepdims=True)
            vn = dv * lax.rsqrt(var + EPS)
            mixed = jnp.dot(sw_ref[g], vn.astype(BF16), preferred_element_type=F32) + sb_ref[g]
            ycat_ref[rows, lru_width + g * group_dim:lru_width + (g + 1) * group_dim] = (
                _gelu(u_ref[0, rows, cols]) * mixed).astype(BF16)
    y = y_lru + jnp.dot(ycat_ref[:, lru_width:], wout_ref[lru_width:, :], preferred_element_type=F32)
    o_ref[0] = x_ref[0] + g1_ref[0] * y


def _mixer(x, proj, y_fwd, y_bwd, sgu_w, sgu_b_bcast, w_out, gate1, t_rows):
    bsz, seq, d = x.shape
    width = y_fwd.shape[2]
    tile3 = lambda c: pl.BlockSpec((1, t_rows, width), lambda b, i: (b, i, c))
    return pl.pallas_call(
        _mixer_kernel,
        grid=(bsz, seq // t_rows),
        in_specs=[
            pl.BlockSpec((1, t_rows, d), lambda b, i: (b, i, 0)),
            tile3(1), tile3(2), tile3(3),
            tile3(0), tile3(0),
            pl.BlockSpec(sgu_w.shape, lambda b, i: (0, 0, 0)),
            pl.BlockSpec(sgu_b_bcast.shape, lambda b, i: (0, 0, 0)),
            pl.BlockSpec(w_out.shape, lambda b, i: (0, 0)),
            pl.BlockSpec((1, 1, d), lambda b, i: (b, 0, 0)),
        ],
        out_specs=pl.BlockSpec((1, t_rows, d), lambda b, i: (b, i, 0)),
        out_shape=jax.ShapeDtypeStruct((bsz, seq, d), F32),
        scratch_shapes=[pltpu.VMEM((t_rows, w_out.shape[0]), BF16)],
        compiler_params=_params("arbitrary", "arbitrary"),
        name="mixer",
    )(x, proj, proj, proj, y_fwd, y_bwd, sgu_w, sgu_b_bcast, w_out, gate1)


def _oddeven_mergesort_pairs(n):
    pairs = []
    p = 1
    while p < n:
        k = p
        while k >= 1:
            for j in range(k % p, n - k, 2 * k):
                for i in range(min(k, n - j - k)):
                    if (i + j) // (2 * p) == (i + j + k) // (2 * p):
                        pairs.append((i + j, i + j + k))
            k //= 2
        p *= 2
    return pairs


def _bitonic_merge_pairs(n):
    pairs = []
    d = n // 2
    while d >= 1:
        pairs.extend((i, i + d) for i in range(n) if not i & d)
        d //= 2
    return pairs


def _compare_exchange(v, pairs):
    for i, j in pairs:
        a, b = v[i], v[j]
        if b is None:
            continue
        if a is None:
            v[i], v[j] = b, None
        else:
            v[i], v[j] = jnp.maximum(a, b), jnp.minimum(a, b)


def _top16_groups(groups):
    v = list(groups)
    n = len(v)
    _compare_exchange(v, _oddeven_mergesort_pairs(n))
    shift = V7X_SUBLANES // 2
    while shift >= 1:
        merged = []
        for i in range(n):
            a, b = v[i], v[n - 1 - i]
            b = None if b is None else pltpu.roll(b, shift, 0)
            merged.append(b if a is None else (a if b is None else jnp.maximum(a, b)))
        v = merged
        _compare_exchange(v, _bitonic_merge_pairs(n))
        shift //= 2
    return v


def _sublane_gather(groups, sub):
    out = groups[-1]
    for j in range(len(groups) - 2, -1, -1):
        out = jnp.where(sub == j, groups[j], out)
    return out


def _pair_candidates(v1, v2, sub):
    ns = V7X_SUBLANES
    neg_inf = jnp.float32(-jnp.inf)
    v2_lo = _sublane_gather(v2[:ns], sub)
    v2_hi = _sublane_gather(v2[ns:], sub)
    v1_hi = _sublane_gather(v1[ns:], sub)
    groups = [v1[0] + v2_lo, v1[0] + v2_hi, v1[1] + v2_lo]
    for i in range(2, ns):
        groups.append(jnp.where(sub < PEER_TOPK // (i + 1), v1[i] + v2_lo, neg_inf))
    groups.append(v1_hi + v2[0])
    return groups


def _monotone_count(v, pred):
    w = jnp.where
    c1 = pred(v[7])
    c2 = pred(w(c1, v[11], v[3]))
    c3 = pred(w(c1, w(c2, v[13], v[9]), w(c2, v[5], v[1])))
    c4 = pred(w(c1, w(c2, w(c3, v[14], v[12]), w(c3, v[10], v[8])),
                w(c2, w(c3, v[6], v[4]), w(c3, v[2], v[0]))))
    c5 = pred(v[15])
    return (w(c1, 8.0, 0.0) + w(c2, 4.0, 0.0)) + (w(c3, 2.0, 0.0) + w(c4, 1.0, 0.0)) + w(c5, 1.0, 0.0)


def _bf16_pair_words(x):
    bits = pltpu.bitcast(x.astype(BF16).astype(F32), U32)
    return bits | lax.shift_right_logical(bits, jnp.uint32(16))


def _peer_route_kernel(x_ref, g_ref, shift_ref, scale_ref, wqt_ref, k1_ref, k2_ref,
                       ht_ref, rank2_ref, p2_ref, cntw_ref, p1w_ref):
    heads, n_keys, half = k1_ref.shape
    n_tok = x_ref.shape[0]
    ns = V7X_SUBLANES
    n_groups = n_keys // ns
    hn = _norm_modulate(x_ref[...], g_ref[...], shift_ref[0], scale_ref[0])
    ht = hn.T.astype(BF16)
    ht_ref[...] = ht
    qt = jnp.dot(wqt_ref[...], ht, preferred_element_type=F32)
    lanes = V7X_LANES
    sub = lax.broadcasted_iota(jnp.int32, (ns, lanes), 0)
    for h in range(heads):
        q1 = qt[(2 * h) * half:(2 * h + 1) * half].astype(BF16)
        q2 = qt[(2 * h + 1) * half:(2 * h + 2) * half].astype(BF16)
        s1_all = jnp.dot(k1_ref[h], q1, preferred_element_type=F32)
        s2_all = jnp.dot(k2_ref[h], q2, preferred_element_type=F32)
        for c in range(n_tok // lanes):
            cols = slice(c * lanes, (c + 1) * lanes)
            s1, s2 = s1_all[:, cols], s2_all[:, cols]
            s1g = s1.reshape(n_groups, ns, lanes)
            s2g = s2.reshape(n_groups, ns, lanes)
            v1 = _top16_groups([s1g[i] for i in range(n_groups)])
            v2 = _top16_groups([s2g[i] for i in range(n_groups)])
            cand = _pair_candidates(v1, v2, sub)
            tau = _top16_groups(cand + [None] * (n_groups - len(cand)))[PEER_TOPK - 1]
            m = v1[0] + v2[0]
            zs = None
            for cg in cand:
                term = jnp.where(cg >= tau, jnp.exp(cg - m), 0.0)
                zs = term if zs is None else zs + term
            inv_z = 1.0 / jnp.sum(zs, axis=0, keepdims=True)
            v2b = [v[None] for v in v2]
            rank2 = _monotone_count(v2b, lambda t: t > s2g)
            cnt = _monotone_count(v2b, lambda t: s1g + t >= tau[None])
            rank2_ref[h, :, cols] = rank2.reshape(n_keys, lanes).astype(BF16)
            p2_ref[h, :, cols] = jnp.exp(s2 - v2[0][0:1]).astype(BF16)
            cntw_ref[h, :, cols] = _bf16_pair_words(cnt.reshape(n_keys, lanes))
            p1w_ref[h, :, cols] = _bf16_pair_words(jnp.exp(s1 - v1[0][0:1]) * inv_z)


def _peer_route(x2, g, shift, scale, wq_t, k1, k2, rows_per_vec, tm):
    n, d = x2.shape
    heads, n_keys, _ = k1.shape
    tiles_per_vec = rows_per_vec // tm
    tok_block = pl.BlockSpec((heads, n_keys, tm), lambda i: (0, 0, i))
    return pl.pallas_call(
        _peer_route_kernel,
        grid=(n // tm,),
        in_specs=[
            pl.BlockSpec((tm, d), lambda i: (i, 0)),
            pl.BlockSpec((1, d), lambda i: (0, 0)),
            pl.BlockSpec((1, 1, d), lambda i: (i // tiles_per_vec, 0, 0)),
            pl.BlockSpec((1, 1, d), lambda i: (i // tiles_per_vec, 0, 0)),
            pl.BlockSpec(wq_t.shape, lambda i: (0, 0)),
            pl.BlockSpec(k1.shape, lambda i: (0, 0, 0)),
            pl.BlockSpec(k2.shape, lambda i: (0, 0, 0)),
        ],
        out_specs=[pl.BlockSpec((d, tm), lambda i: (0, i)), tok_block, tok_block, tok_block, tok_block],
        out_shape=[
            jax.ShapeDtypeStruct((d, n), BF16),
            jax.ShapeDtypeStruct((heads, n_keys, n), BF16),
            jax.ShapeDtypeStruct((heads, n_keys, n), BF16),
            jax.ShapeDtypeStruct((heads, n_keys, n), U32),
            jax.ShapeDtypeStruct((heads, n_keys, n), U32),
        ],
        compiler_params=_params("arbitrary"),
        name="peer_route",
    )(x2, g, shift, scale, wq_t, k1, k2)


def _peer_experts_kernel(ht_ref, rank2_ref, p2_ref, cntw_ref, p1w_ref, u_ref, vt_ref, o_ref):
    e = pl.program_id(1)
    heads, n_keys, n_tok = rank2_ref.shape
    e1_per_block = u_ref.shape[0] // n_keys
    ts = V7X_MXU_COLS
    nb = V7X_BF16_ROWS
    ns = V7X_SUBLANES

    @pl.when(e == 0)
    def _():
        o_ref[...] = jnp.zeros_like(o_ref)

    e1_splits = (0, e1_per_block // 2, e1_per_block)

    def pre_activation(blk):
        cols, lo, hi = blk
        return jnp.dot(u_ref[lo * n_keys:hi * n_keys, :], ht_ref[:, cols],
                       preferred_element_type=F32)

    def weighted_activation(a_t, blk):
        cols, lo, hi = blk
        blocks = []
        for j in range(hi - lo):
            e1 = e * e1_per_block + lo + j
            cnt = [pltpu.bitcast(jnp.broadcast_to(cntw_ref[h, pl.ds(e1, 1), cols], (ns, ts)), BF16)
                   for h in range(heads)]
            p1 = [pltpu.bitcast(jnp.broadcast_to(p1w_ref[h, pl.ds(e1, 1), cols], (ns, ts)), BF16)
                  for h in range(heads)]
            for rg in range(n_keys // nb):
                rows = slice(rg * nb, (rg + 1) * nb)
                w = None
                for h in range(heads):
                    wh = jnp.where(rank2_ref[h, rows, cols] < cnt[h], p2_ref[h, rows, cols] * p1[h],
                                   jnp.zeros((nb, ts), BF16))
                    w = wh if w is None else w + wh
                act = _gelu(a_t[j * n_keys + rg * nb:j * n_keys + (rg + 1) * nb].astype(BF16))
                blocks.append(w * act)
        return jnp.concatenate(blocks, axis=0)

    def accumulate(wa, blk):
        cols, lo, hi = blk
        o_ref[:, cols] += jnp.dot(vt_ref[:, lo * n_keys:hi * n_keys], wa,
                                  preferred_element_type=F32)

    chains = [(slice(s * ts, (s + 1) * ts), lo, hi) for s in range(n_tok // ts)
              for lo, hi in zip(e1_splits[:-1], e1_splits[1:])]
    n_chain = len(chains)
    pre = {0: pre_activation(chains[0])}
    wa = {}
    for k in range(n_chain + 1):
        if k + 1 < n_chain:
            pre[k + 1] = pre_activation(chains[k + 1])
        if k >= 1:
            accumulate(wa.pop(k - 1), chains[k - 1])
        if k < n_chain:
            wa[k] = weighted_activation(pre.pop(k), chains[k])


def _peer_experts(ht, rank2, p2, cntw, p1w, u_bf, vt_bf, tm, eb):
    d, n = ht.shape
    heads, n_keys, _ = rank2.shape
    n_exp = u_bf.shape[0]
    tok_block = pl.BlockSpec((heads, n_keys, tm), lambda i, e: (0, 0, i))
    return pl.pallas_call(
        _peer_experts_kernel,
        grid=(n // tm, n_exp // eb),
        in_specs=[
            pl.BlockSpec((d, tm), lambda i, e: (0, i)),
            tok_block, tok_block, tok_block, tok_block,
            pl.BlockSpec((eb, d), lambda i, e: (e, 0)),
            pl.BlockSpec((d, eb), lambda i, e: (0, e)),
        ],
        out_specs=pl.BlockSpec((d, tm), lambda i, e: (0, i)),
        out_shape=jax.ShapeDtypeStruct((d, n), F32),
        compiler_params=_params("arbitrary", "arbitrary"),
        name="peer_experts",
    )(ht, rank2, p2, cntw, p1w, u_bf, vt_bf)


def _final_kernel(x_ref, pt_ref, gate_ref, g_ref, o_ref):
    x = x_ref[...] + gate_ref[0] * pt_ref[...].T
    ms = jnp.mean(x * x, axis=-1, keepdims=True)
    o_ref[...] = x * lax.rsqrt(ms + EPS) * g_ref[...]


def _final(x2, peer_t, gate, g, rows_per_vec, tm):
    n, d = x2.shape
    tiles_per_vec = rows_per_vec // tm
    return pl.pallas_call(
        _final_kernel,
        grid=(n // tm,),
        in_specs=[
            pl.BlockSpec((tm, d), lambda i: (i, 0)),
            pl.BlockSpec((d, tm), lambda i: (0, i)),
            pl.BlockSpec((1, 1, d), lambda i: (i // tiles_per_vec, 0, 0)),
            pl.BlockSpec((1, d), lambda i: (0, 0)),
        ],
        out_specs=pl.BlockSpec((tm, d), lambda i: (i, 0)),
        out_shape=jax.ShapeDtypeStruct((n, d), F32),
        compiler_params=_params("arbitrary"),
        name="final_norm",
    )(x2, peer_t, gate, g)


def kernel(x, c, ctx, c_ctx, w_mod, b_mod, norm1_g, norm2_g, w_in, conv_w, conv_b, lru_wa, lru_ba, lru_wx,
           lru_bx, lru_lambda, sgu_w, sgu_b, w_out, peer_wq, peer_k1, peer_k2, peer_u, peer_v, final_g):
    bsz, seq, d = x.shape
    ctx_len = ctx.shape[1]
    depth = w_mod.shape[0]
    assert depth == 1, "only the single-layer configuration of the reference is implemented"
    lru_width = conv_w.shape[2]
    assert seq % 512 == 0 and ctx_len % CHUNK == 0 and d % V7X_LANES == 0

    t_seq = 512
    t_tok = 512
    expert_block = 1024

    w_in_bf = w_in[0].astype(BF16)
    w_out_bf = w_out[0].astype(BF16)
    wa_bf, wx_bf = lru_wa[0].astype(BF16), lru_wx[0].astype(BF16)
    sgu_w_bf = sgu_w[0].astype(BF16)
    sgu_b_bcast = jnp.broadcast_to(sgu_b[0][:, :, None], sgu_b.shape[1:] + (sgu_w.shape[-1],))
    wq_t_bf = peer_wq[0].T.astype(BF16)
    k1_bf, k2_bf = peer_k1[0].astype(BF16), peer_k2[0].astype(BF16)
    u_bf = peer_u[0].astype(BF16)
    vt_bf = peer_v[0].T.astype(BF16)

    mod = _modulation(jnp.concatenate([c, c_ctx[None, :]], axis=0), w_mod[0], b_mod[0])
    mod = mod.reshape(bsz + 1, N_MOD, 1, d)
    lat = lambda k: mod[:bsz, k]
    ctx_vec = lambda k: mod[bsz:, k]
    g1 = norm1_g[0].reshape(1, d)
    g2 = norm2_g[0].reshape(1, d)

    px_ctx = _norm_proj(ctx.reshape(bsz * ctx_len, d), g1, ctx_vec(0), ctx_vec(1), w_in_bf[:, :lru_width],
                        rows_per_vec=bsz * ctx_len, tm=ctx_len).reshape(bsz, ctx_len, lru_width)
    proj = _norm_proj(x.reshape(bsz * seq, d), g1, lat(0), lat(1), w_in_bf,
                      rows_per_vec=seq, tm=1024).reshape(bsz, seq, w_in.shape[2])

    zeros_h = jnp.zeros((bsz, 1, lru_width), F32)
    ys = []
    for direction, reverse in enumerate((False, True)):
        prm = (conv_w[0], conv_b[0], wa_bf[direction], lru_ba[0, direction], wx_bf[direction],
               lru_bx[0, direction], lru_lambda[0, direction])
        y_ctx = _lru_scan(px_ctx, *prm, zeros_h, reverse, ctx_len)
        h_ctx = y_ctx[:, 0:1] if reverse else y_ctx[:, ctx_len - 1:ctx_len]
        ys.append(_lru_scan(proj, *prm, h_ctx, reverse, t_seq))

    x_mid = _mixer(x, proj, ys[0], ys[1], sgu_w_bf, sgu_b_bcast, w_out_bf, lat(2), t_seq)

    x2 = x_mid.reshape(bsz * seq, d)
    ht, rank2, p2, cntw, p1w = _peer_route(x2, g2, lat(3), lat(4), wq_t_bf, k1_bf, k2_bf,
                                           rows_per_vec=seq, tm=t_tok)
    peer_t = _peer_experts(ht, rank2, p2, cntw, p1w, u_bf, vt_bf, t_tok, expert_block)
    out = _final(x2, peer_t, lat(5), final_g.reshape(1, d), rows_per_vec=seq, tm=512)
    return out.reshape(bsz, seq, d)
```

```python
import functools

import jax
import jax.numpy as jnp
from jax import lax
from jax.experimental import pallas as pl
from jax.experimental.pallas import tpu as pltpu

F32 = jnp.float32
BF16 = jnp.bfloat16
U32 = jnp.uint32

EPS = 1e-6
LRU_C = 8.0
CONV_LEFT = 2
CHUNK = 128
PEER_TOPK = 16
N_MOD = 6

V7X_LANES = 128
V7X_SUBLANES = 8
V7X_BF16_ROWS = 16
V7X_MXU_COLS = 256
V7X_VMEM_LIMIT_BYTES = 56 * 1024 * 1024


def _params(*semantics):
    return pltpu.CompilerParams(dimension_semantics=semantics, vmem_limit_bytes=V7X_VMEM_LIMIT_BYTES)


def _gelu(x):
    return 0.5 * x * (1.0 + jnp.tanh(0.7978845608028654 * (x + 0.044715 * (x * x * x))))


def _norm_modulate(x, g, shift, scale):
    ms = jnp.mean(x * x, axis=-1, keepdims=True)
    return (x * lax.rsqrt(ms + EPS) * g) * (1.0 + scale) + shift


def _modulation_kernel(cs_ref, w_ref, b_ref, o_ref):
    n_rows = cs_ref.shape[0]
    nb = w_ref.shape[1]
    rows = []
    for r in range(n_rows):
        c = cs_ref[r]
        s = c * jax.nn.sigmoid(c)
        parts = [jnp.sum(w_ref[:, n * V7X_LANES:(n + 1) * V7X_LANES] * s, axis=0, keepdims=True)
                 for n in range(nb // V7X_LANES)]
        rows.append(jnp.concatenate(parts, axis=1) + b_ref[...])
    rows.append(jnp.zeros((V7X_SUBLANES - n_rows, nb), F32))
    o_ref[...] = jnp.concatenate(rows, axis=0)


def _modulation(cs, w_mod, b_mod):
    n_rows, d = cs.shape
    n = w_mod.shape[1]
    nb = 1024
    csb = jnp.broadcast_to(cs[:, :, None], (n_rows, d, V7X_LANES))
    out = pl.pallas_call(
        _modulation_kernel,
        grid=(n // nb,),
        in_specs=[
            pl.BlockSpec((n_rows, d, V7X_LANES), lambda j: (0, 0, 0)),
            pl.BlockSpec((d, nb), lambda j: (0, j)),
            pl.BlockSpec((1, nb), lambda j: (0, j)),
        ],
        out_specs=pl.BlockSpec((V7X_SUBLANES, nb), lambda j: (0, j)),
        out_shape=jax.ShapeDtypeStruct((V7X_SUBLANES, n), F32),
        compiler_params=_params("arbitrary"),
        name="modulation",
    )(csb, w_mod, b_mod.reshape(1, n))
    return out[:n_rows]


def _norm_proj_kernel(x_ref, g_ref, shift_ref, scale_ref, w_ref, o_ref):
    hn = _norm_modulate(x_ref[...], g_ref[...], shift_ref[0], scale_ref[0])
    o_ref[...] = jnp.dot(hn.astype(BF16), w_ref[...], preferred_element_type=F32)


def _norm_proj(x2, g, shift, scale, w, rows_per_vec, tm):
    r, d = x2.shape
    n = w.shape[1]
    tn = min(n, 2048)
    tiles_per_vec = rows_per_vec // tm
    return pl.pallas_call(
        _norm_proj_kernel,
        grid=(r // tm, n // tn),
        in_specs=[
            pl.BlockSpec((tm, d), lambda i, j: (i, 0)),
            pl.BlockSpec((1, d), lambda i, j: (0, 0)),
            pl.BlockSpec((1, 1, d), lambda i, j: (i // tiles_per_vec, 0, 0)),
            pl.BlockSpec((1, 1, d), lambda i, j: (i // tiles_per_vec, 0, 0)),
            pl.BlockSpec((d, tn), lambda i, j: (0, j)),
        ],
        out_specs=pl.BlockSpec((tm, tn), lambda i, j: (i, j)),
        out_shape=jax.ShapeDtypeStruct((r, n), F32),
        compiler_params=_params("arbitrary", "arbitrary"),
        name="norm_proj",
    )(x2, g, shift, scale, w)


def _lru_scan_kernel(cur_ref, prev_ref, next_ref, cw_ref, cb_ref, wa_ref, ba_ref, wx_ref, bx_ref,
                     lam_ref, h0_ref, y_ref, xe_ref, a_ref, b_ref, h_ref, *, reverse, n_tiles):
    i = pl.program_id(1)
    ti = (n_tiles - 1 - i) if reverse else i
    t_rows, width = a_ref.shape
    heads, head_dim = wa_ref.shape[0], wa_ref.shape[1]
    halo = V7X_SUBLANES

    @pl.when(i == 0)
    def _():
        h_ref[...] = h0_ref[0]

    zero = jnp.zeros((halo, width), F32)
    xe_ref[0:halo, :] = jnp.where(ti == 0, zero, prev_ref[0])
    xe_ref[halo:halo + t_rows, :] = cur_ref[0]
    xe_ref[halo + t_rows:, :] = jnp.where(ti == n_tiles - 1, zero, next_ref[0])

    xc = jnp.broadcast_to(cb_ref[...], (t_rows, width))
    for k in range(cw_ref.shape[0]):
        lo = halo - CONV_LEFT + k
        xc = xc + xe_ref[lo:lo + t_rows, :] * cw_ref[k:k + 1, :]

    xcb = xc.astype(BF16)
    r_parts, i_parts = [], []
    for h in range(heads):
        xh = xcb[:, h * head_dim:(h + 1) * head_dim]
        r_parts.append(jnp.dot(xh, wa_ref[h], preferred_element_type=F32))
        i_parts.append(jnp.dot(xh, wx_ref[h], preferred_element_type=F32))
    r = 0.5 * (1.0 + jnp.tanh(0.5 * (jnp.concatenate(r_parts, axis=1) + ba_ref[...])))
    ig = 0.5 * (1.0 + jnp.tanh(0.5 * (jnp.concatenate(i_parts, axis=1) + bx_ref[...])))
    lam = lam_ref[...]
    softplus_neg_lam = jnp.maximum(-lam, 0.0) + jnp.log1p(jnp.exp(-jnp.abs(lam)))
    log_a = -LRU_C * r * softplus_neg_lam
    a_ref[...] = jnp.exp(log_a)
    th = jnp.tanh(log_a)
    b_ref[...] = (jnp.sqrt(-2.0 * th) * lax.rsqrt(1.0 - th)) * (ig * xc)

    def step(s, h):
        t = (t_rows - 1 - s) if reverse else s
        h = a_ref[pl.ds(t, 1), :] * h + b_ref[pl.ds(t, 1), :]
        y_ref[0, pl.ds(t, 1), :] = h
        return h

    h_ref[...] = lax.fori_loop(0, t_rows, step, h_ref[...], unroll=8)


def _lru_scan(px, conv_w, conv_b, wa, ba, wx, bx, lam, h0, reverse, t_rows):
    bsz, seq, _ = px.shape
    width = conv_w.shape[1]
    n_tiles = seq // t_rows
    halo = V7X_SUBLANES
    tb = t_rows // halo
    last_halo_block = seq // halo - 1

    def tile(i):
        return (n_tiles - 1 - i) if reverse else i

    vec = lambda b, i: (0, 0)
    kern = functools.partial(_lru_scan_kernel, reverse=reverse, n_tiles=n_tiles)
    return pl.pallas_call(
        kern,
        grid=(bsz, n_tiles),
        in_specs=[
            pl.BlockSpec((1, t_rows, width), lambda b, i: (b, tile(i), 0)),
            pl.BlockSpec((1, halo, width), lambda b, i: (b, jnp.maximum(tile(i) * tb - 1, 0), 0)),
            pl.BlockSpec((1, halo, width), lambda b, i: (b, jnp.minimum((tile(i) + 1) * tb, last_halo_block), 0)),
            pl.BlockSpec(conv_w.shape, vec),
            pl.BlockSpec((1, width), vec),
            pl.BlockSpec(wa.shape, lambda b, i: (0, 0, 0)),
            pl.BlockSpec((1, width), vec),
            pl.BlockSpec(wx.shape, lambda b, i: (0, 0, 0)),
            pl.BlockSpec((1, width), vec),
            pl.BlockSpec((1, width), vec),
            pl.BlockSpec((1, 1, width), lambda b, i: (b, 0, 0)),
        ],
        out_specs=pl.BlockSpec((1, t_rows, width), lambda b, i: (b, tile(i), 0)),
        out_shape=jax.ShapeDtypeStruct((bsz, seq, width), F32),
        scratch_shapes=[
            pltpu.VMEM((t_rows + 2 * halo, width), F32),
            pltpu.VMEM((t_rows, width), F32),
            pltpu.VMEM((t_rows, width), F32),
            pltpu.VMEM((1, width), F32),
        ],
        compiler_params=_params("arbitrary", "arbitrary"),
        name="lru_scan_bwd" if reverse else "lru_scan_fwd",
    )(px, px, px, conv_w, conv_b.reshape(1, width), wa, ba.reshape(1, width), wx, bx.reshape(1, width),
      lam.reshape(1, width), h0)


def _mixer_kernel(x_ref, gate_ref, u_ref, v_ref, yf_ref, yb_ref, sw_ref, sb_ref, wout_ref, g1_ref, o_ref,
                  ycat_ref):
    t_rows = x_ref.shape[1]
    lru_width = gate_ref.shape[2]
    groups, group_dim = sw_ref.shape[0], u_ref.shape[2] // sw_ref.shape[0]

    ycat_ref[:, 0:lru_width] = (_gelu(gate_ref[0]) * (yf_ref[0] + yb_ref[0])).astype(BF16)
    y_lru = jnp.dot(ycat_ref[:, 0:lru_width], wout_ref[0:lru_width, :], preferred_element_type=F32)
    for n in range(t_rows // CHUNK):
        rows = slice(n * CHUNK, (n + 1) * CHUNK)
        for g in range(groups):
            cols = slice(g * group_dim, (g + 1) * group_dim)
            vg = _gelu(v_ref[0, rows, cols])
            mu = jnp.mean(vg, axis=-1, keepdims=True)
            dv = vg - mu
            var = jnp.mean(dv * dv, axis=-1, keepdims=True)
            vn = dv * lax.rsqrt(var + EPS)
            mixed = jnp.dot(sw_ref[g], vn.astype(BF16), preferred_element_type=F32) + sb_ref[g]
            ycat_ref[rows, lru_width + g * group_dim:lru_width + (g + 1) * group_dim] = (
                _gelu(u_ref[0, rows, cols]) * mixed).astype(BF16)
    y = y_lru + jnp.dot(ycat_ref[:, lru_width:], wout_ref[lru_width:, :], preferred_element_type=F32)
    o_ref[0] = x_ref[0] + g1_ref[0] * y


def _mixer(x, proj, y_fwd, y_bwd, sgu_w, sgu_b_bcast, w_out, gate1, t_rows):
    bsz, seq, d = x.shape
    width = y_fwd.shape[2]
    tile3 = lambda c: pl.BlockSpec((1, t_rows, width), lambda b, i: (b, i, c))
    return pl.pallas_call(
        _mixer_kernel,
        grid=(bsz, seq // t_rows),
        in_specs=[
            pl.BlockSpec((1, t_rows, d), lambda b, i: (b, i, 0)),
            tile3(1), tile3(2), tile3(3),
            tile3(0), tile3(0),
            pl.BlockSpec(sgu_w.shape, lambda b, i: (0, 0, 0)),
            pl.BlockSpec(sgu_b_bcast.shape, lambda b, i: (0, 0, 0)),
            pl.BlockSpec(w_out.shape, lambda b, i: (0, 0)),
            pl.BlockSpec((1, 1, d), lambda b, i: (b, 0, 0)),
        ],
        out_specs=pl.BlockSpec((1, t_rows, d), lambda b, i: (b, i, 0)),
        out_shape=jax.ShapeDtypeStruct((bsz, seq, d), F32),
        scratch_shapes=[pltpu.VMEM((t_rows, w_out.shape[0]), BF16)],
        compiler_params=_params("arbitrary", "arbitrary"),
        name="mixer",
    )(x, proj, proj, proj, y_fwd, y_bwd, sgu_w, sgu_b_bcast, w_out, gate1)


def _oddeven_mergesort_pairs(n):
    pairs = []
    p = 1
    while p < n:
        k = p
        while k >= 1:
            for j in range(k % p, n - k, 2 * k):
                for i in range(min(k, n - j - k)):
                    if (i + j) // (2 * p) == (i + j + k) // (2 * p):
                        pairs.append((i + j, i + j + k))
            k //= 2
        p *= 2
    return pairs


def _bitonic_merge_pairs(n):
    pairs = []
    d = n // 2
    while d >= 1:
        pairs.extend((i, i + d) for i in range(n) if not i & d)
        d //= 2
    return pairs


def _compare_exchange(v, pairs):
    for i, j in pairs:
        a, b = v[i], v[j]
        if b is None:
            continue
        if a is None:
            v[i], v[j] = b, None
        else:
            v[i], v[j] = jnp.maximum(a, b), jnp.minimum(a, b)


def _top16_groups(groups):
    v = list(groups)
    n = len(v)
    _compare_exchange(v, _oddeven_mergesort_pairs(n))
    shift = V7X_SUBLANES // 2
    while shift >= 1:
        merged = []
        for i in range(n):
            a, b = v[i], v[n - 1 - i]
            b = None if b is None else pltpu.roll(b, shift, 0)
            merged.append(b if a is None else (a if b is None else jnp.maximum(a, b)))
        v = merged
        _compare_exchange(v, _bitonic_merge_pairs(n))
        shift //= 2
    return v


def _sublane_gather(groups, sub):
    out = groups[-1]
    for j in range(len(groups) - 2, -1, -1):
        out = jnp.where(sub == j, groups[j], out)
    return out


def _pair_candidates(v1, v2, sub):
    ns = V7X_SUBLANES
    neg_inf = jnp.float32(-jnp.inf)
    v2_lo = _sublane_gather(v2[:ns], sub)
    v2_hi = _sublane_gather(v2[ns:], sub)
    v1_hi = _sublane_gather(v1[ns:], sub)
    groups = [v1[0] + v2_lo, v1[0] + v2_hi, v1[1] + v2_lo]
    for i in range(2, ns):
        groups.append(jnp.where(sub < PEER_TOPK // (i + 1), v1[i] + v2_lo, neg_inf))
    groups.append(v1_hi + v2[0])
    return groups


def _monotone_count(v, pred):
    w = jnp.where
    c1 = pred(v[7])
    c2 = pred(w(c1, v[11], v[3]))
    c3 = pred(w(c1, w(c2, v[13], v[9]), w(c2, v[5], v[1])))
    c4 = pred(w(c1, w(c2, w(c3, v[14], v[12]), w(c3, v[10], v[8])),
                w(c2, w(c3, v[6], v[4]), w(c3, v[2], v[0]))))
    c5 = pred(v[15])
    return (w(c1, 8.0, 0.0) + w(c2, 4.0, 0.0)) + (w(c3, 2.0, 0.0) + w(c4, 1.0, 0.0)) + w(c5, 1.0, 0.0)


def _bf16_pair_words(x):
    bits = pltpu.bitcast(x.astype(BF16).astype(F32), U32)
    return bits | lax.shift_right_logical(bits, jnp.uint32(16))


def _peer_route_kernel(x_ref, g_ref, shift_ref, scale_ref, wqt_ref, k1_ref, k2_ref,
                       ht_ref, rank2_ref, p2_ref, cntw_ref, p1w_ref):
    heads, n_keys, half = k1_ref.shape
    n_tok = x_ref.shape[0]
    ns = V7X_SUBLANES
    n_groups = n_keys // ns
    hn = _norm_modulate(x_ref[...], g_ref[...], shift_ref[0], scale_ref[0])
    ht = hn.T.astype(BF16)
    ht_ref[...] = ht
    qt = jnp.dot(wqt_ref[...], ht, preferred_element_type=F32)
    lanes = V7X_LANES
    sub = lax.broadcasted_iota(jnp.int32, (ns, lanes), 0)
    for h in range(heads):
        q1 = qt[(2 * h) * half:(2 * h + 1) * half].astype(BF16)
        q2 = qt[(2 * h + 1) * half:(2 * h + 2) * half].astype(BF16)
        s1_all = jnp.dot(k1_ref[h], q1, preferred_element_type=F32)
        s2_all = jnp.dot(k2_ref[h], q2, preferred_element_type=F32)
        for c in range(n_tok // lanes):
            cols = slice(c * lanes, (c + 1) * lanes)
            s1, s2 = s1_all[:, cols], s2_all[:, cols]
            s1g = s1.reshape(n_groups, ns, lanes)
            s2g = s2.reshape(n_groups, ns, lanes)
            v1 = _top16_groups([s1g[i] for i in range(n_groups)])
            v2 = _top16_groups([s2g[i] for i in range(n_groups)])
            cand = _pair_candidates(v1, v2, sub)
            tau = _top16_groups(cand + [None] * (n_groups - len(cand)))[PEER_TOPK - 1]
            m = v1[0] + v2[0]
            zs = None
            for cg in cand:
                term = jnp.where(cg >= tau, jnp.exp(cg - m), 0.0)
                zs = term if zs is None else zs + term
            inv_z = 1.0 / jnp.sum(zs, axis=0, keepdims=True)
            v2b = [v[None] for v in v2]
            rank2 = _monotone_count(v2b, lambda t: t > s2g)
            cnt = _monotone_count(v2b, lambda t: s1g + t >= tau[None])
            rank2_ref[h, :, cols] = rank2.reshape(n_keys, lanes).astype(BF16)
            p2_ref[h, :, cols] = jnp.exp(s2 - v2[0][0:1]).astype(BF16)
            cntw_ref[h, :, cols] = _bf16_pair_words(cnt.reshape(n_keys, lanes))
            p1w_ref[h, :, cols] = _bf16_pair_words(jnp.exp(s1 - v1[0][0:1]) * inv_z)


def _peer_route(x2, g, shift, scale, wq_t, k1, k2, rows_per_vec, tm):
    n, d = x2.shape
    heads, n_keys, _ = k1.shape
    tiles_per_vec = rows_per_vec // tm
    tok_block = pl.BlockSpec((heads, n_keys, tm), lambda i: (0, 0, i))
    return pl.pallas_call(
        _peer_route_kernel,
        grid=(n // tm,),
        in_specs=[
            pl.BlockSpec((tm, d), lambda i: (i, 0)),
            pl.BlockSpec((1, d), lambda i: (0, 0)),
            pl.BlockSpec((1, 1, d), lambda i: (i // tiles_per_vec, 0, 0)),
            pl.BlockSpec((1, 1, d), lambda i: (i // tiles_per_vec, 0, 0)),
            pl.BlockSpec(wq_t.shape, lambda i: (0, 0)),
            pl.BlockSpec(k1.shape, lambda i: (0, 0, 0)),
            pl.BlockSpec(k2.shape, lambda i: (0, 0, 0)),
        ],
        out_specs=[pl.BlockSpec((d, tm), lambda i: (0, i)), tok_block, tok_block, tok_block, tok_block],
        out_shape=[
            jax.ShapeDtypeStruct((d, n), BF16),
            jax.ShapeDtypeStruct((heads, n_keys, n), BF16),
            jax.ShapeDtypeStruct((heads, n_keys, n), BF16),
            jax.ShapeDtypeStruct((heads, n_keys, n), U32),
            jax.ShapeDtypeStruct((heads, n_keys, n), U32),
        ],
        compiler_params=_params("arbitrary"),
        name="peer_route",
    )(x2, g, shift, scale, wq_t, k1, k2)


def _peer_experts_kernel(ht_ref, rank2_ref, p2_ref, cntw_ref, p1w_ref, u_ref, vt_ref, o_ref):
    e = pl.program_id(1)
    heads, n_keys, n_tok = rank2_ref.shape
    e1_per_block = u_ref.shape[0] // n_keys
    ts = V7X_MXU_COLS
    nb = V7X_BF16_ROWS
    ns = V7X_SUBLANES

    @pl.when(e == 0)
    def _():
        o_ref[...] = jnp.zeros_like(o_ref)

    e1_splits = (0, e1_per_block // 2, e1_per_block)

    def pre_activation(blk):
        cols, lo, hi = blk
        return jnp.dot(u_ref[lo * n_keys:hi * n_keys, :], ht_ref[:, cols],
                       preferred_element_type=F32)

    def weighted_activation(a_t, blk):
        cols, lo, hi = blk
        blocks = []
        for j in range(hi - lo):
            e1 = e * e1_per_block + lo + j
            cnt = [pltpu.bitcast(jnp.broadcast_to(cntw_ref[h, pl.ds(e1, 1), cols], (ns, ts)), BF16)
                   for h in range(heads)]
            p1 = [pltpu.bitcast(jnp.broadcast_to(p1w_ref[h, pl.ds(e1, 1), cols], (ns, ts)), BF16)
                  for h in range(heads)]
            for rg in range(n_keys // nb):
                rows = slice(rg * nb, (rg + 1) * nb)
                w = None
                for h in range(heads):
                    wh = jnp.where(rank2_ref[h, rows, cols] < cnt[h], p2_ref[h, rows, cols] * p1[h],
                                   jnp.zeros((nb, ts), BF16))
                    w = wh if w is None else w + wh
                act = _gelu(a_t[j * n_keys + rg * nb:j * n_keys + (rg + 1) * nb].astype(BF16))
                blocks.append(w * act)
        return jnp.concatenate(blocks, axis=0)

    def accumulate(wa, blk):
        cols, lo, hi = blk
        o_ref[:, cols] += jnp.dot(vt_ref[:, lo * n_keys:hi * n_keys], wa,
                                  preferred_element_type=F32)

    chains = [(slice(s * ts, (s + 1) * ts), lo, hi) for s in range(n_tok // ts)
              for lo, hi in zip(e1_splits[:-1], e1_splits[1:])]
    n_chain = len(chains)
    pre = {0: pre_activation(chains[0])}
    wa = {}
    for k in range(n_chain + 1):
        if k + 1 < n_chain:
            pre[k + 1] = pre_activation(chains[k + 1])
        if k >= 1:
            accumulate(wa.pop(k - 1), chains[k - 1])
        if k < n_chain:
            wa[k] = weighted_activation(pre.pop(k), chains[k])


def _peer_experts(ht, rank2, p2, cntw, p1w, u_bf, vt_bf, tm, eb):
    d, n = ht.shape
    heads, n_keys, _ = rank2.shape
    n_exp = u_bf.shape[0]
    tok_block = pl.BlockSpec((heads, n_keys, tm), lambda i, e: (0, 0, i))
    return pl.pallas_call(
        _peer_experts_kernel,
        grid=(n // tm, n_exp // eb),
        in_specs=[
            pl.BlockSpec((d, tm), lambda i, e: (0, i)),
            tok_block, tok_block, tok_block, tok_block,
            pl.BlockSpec((eb, d), lambda i, e: (e, 0)),
            pl.BlockSpec((d, eb), lambda i, e: (0, e)),
        ],
        out_specs=pl.BlockSpec((d, tm), lambda i, e: (0, i)),
        out_shape=jax.ShapeDtypeStruct((d, n), F32),
        compiler_params=_params("arbitrary", "arbitrary"),
        name="peer_experts",
    )(ht, rank2, p2, cntw, p1w, u_bf, vt_bf)


def _final_kernel(x_ref, pt_ref, gate_ref, g_ref, o_ref):
    x = x_ref[...] + gate_ref[0] * pt_ref[...].T
    ms = jnp.mean(x * x, axis=-1, keepdims=True)
    o_ref[...] = x * lax.rsqrt(ms + EPS) * g_ref[...]


def _final(x2, peer_t, gate, g, rows_per_vec, tm):
    n, d = x2.shape
    tiles_per_vec = rows_per_vec // tm
    return pl.pallas_call(
        _final_kernel,
        grid=(n // tm,),
        in_specs=[
            pl.BlockSpec((tm, d), lambda i: (i, 0)),
            pl.BlockSpec((d, tm), lambda i: (0, i)),
            pl.BlockSpec((1, 1, d), lambda i: (i // tiles_per_vec, 0, 0)),
            pl.BlockSpec((1, d), lambda i: (0, 0)),
        ],
        out_specs=pl.BlockSpec((tm, d), lambda i: (i, 0)),
        out_shape=jax.ShapeDtypeStruct((n, d), F32),
        compiler_params=_params("arbitrary"),
        name="final_norm",
    )(x2, peer_t, gate, g)


def kernel(x, c, ctx, c_ctx, w_mod, b_mod, norm1_g, norm2_g, w_in, conv_w, conv_b, lru_wa, lru_ba, lru_wx,
           lru_bx, lru_lambda, sgu_w, sgu_b, w_out, peer_wq, peer_k1, peer_k2, peer_u, peer_v, final_g):
    bsz, seq, d = x.shape
    ctx_len = ctx.shape[1]
    depth = w_mod.shape[0]
    assert depth == 1, "only the single-layer configuration of the reference is implemented"
    lru_width = conv_w.shape[2]
    assert seq % 1024 == 0 and ctx_len % CHUNK == 0 and d % V7X_LANES == 0

    t_seq = 512
    t_scan = 1024
    t_tok = 512
    expert_block = 1024

    w_in_bf = w_in[0].astype(BF16)
    w_out_bf = w_out[0].astype(BF16)
    wa_bf, wx_bf = lru_wa[0].astype(BF16), lru_wx[0].astype(BF16)
    sgu_w_bf = sgu_w[0].astype(BF16)
    sgu_b_bcast = jnp.broadcast_to(sgu_b[0][:, :, None], sgu_b.shape[1:] + (sgu_w.shape[-1],))
    wq_t_bf = peer_wq[0].T.astype(BF16)
    k1_bf, k2_bf = peer_k1[0].astype(BF16), peer_k2[0].astype(BF16)
    u_bf = peer_u[0].astype(BF16)
    vt_bf = peer_v[0].T.astype(BF16)

    mod = _modulation(jnp.concatenate([c, c_ctx[None, :]], axis=0), w_mod[0], b_mod[0])
    mod = mod.reshape(bsz + 1, N_MOD, 1, d)
    lat = lambda k: mod[:bsz, k]
    ctx_vec = lambda k: mod[bsz:, k]
    g1 = norm1_g[0].reshape(1, d)
    g2 = norm2_g[0].reshape(1, d)

    px_ctx = _norm_proj(ctx.reshape(bsz * ctx_len, d), g1, ctx_vec(0), ctx_vec(1), w_in_bf[:, :lru_width],
                        rows_per_vec=bsz * ctx_len, tm=ctx_len).reshape(bsz, ctx_len, lru_width)
    proj = _norm_proj(x.reshape(bsz * seq, d), g1, lat(0), lat(1), w_in_bf,
                      rows_per_vec=seq, tm=1024).reshape(bsz, seq, w_in.shape[2])

    zeros_h = jnp.zeros((bsz, 1, lru_width), F32)
    ys = []
    for direction, reverse in enumerate((False, True)):
        prm = (conv_w[0], conv_b[0], wa_bf[direction], lru_ba[0, direction], wx_bf[direction],
               lru_bx[0, direction], lru_lambda[0, direction])
        y_ctx = _lru_scan(px_ctx, *prm, zeros_h, reverse, ctx_len)
        h_ctx = y_ctx[:, 0:1] if reverse else y_ctx[:, ctx_len - 1:ctx_len]
        ys.append(_lru_scan(proj, *prm, h_ctx, reverse, t_scan))

    x_mid = _mixer(x, proj, ys[0], ys[1], sgu_w_bf, sgu_b_bcast, w_out_bf, lat(2), t_seq)

    x2 = x_mid.reshape(bsz * seq, d)
    ht, rank2, p2, cntw, p1w = _peer_route(x2, g2, lat(3), lat(4), wq_t_bf, k1_bf, k2_bf,
                                           rows_per_vec=seq, tm=t_tok)
    peer_t = _peer_experts(ht, rank2, p2, cntw, p1w, u_bf, vt_bf, t_tok, expert_block)
    out = _final(x2, peer_t, lat(5), final_g.reshape(1, d), rows_per_vec=seq, tm=512)
    return out.reshape(bsz, seq, d)
```
